```python
import jax, jax.numpy as jnp
from jax import lax
import numpy as np

D_MODEL = 1024
BATCH = 8
SEQ = 4096
DEPTH = 4

RW_HEADS = 16
RW_HEAD_DIM = 64
RW_WIDTH = RW_HEADS * RW_HEAD_DIM
RW_DECAY_LORA = 64
RW_AAA_LORA = 64
RW_MV_LORA = 32
RW_GATE_LORA = 128
RW_LN_EPS = 64e-5

GM_GROUPS = 8
GM_CHUNK = 128
GM_WIDTH = 1024
GM_GROUP_DIM = GM_WIDTH // GM_GROUPS
GM_LN_EPS = 1e-5

C_IN = 3 * RW_WIDTH + RW_DECAY_LORA + RW_AAA_LORA + RW_GATE_LORA + 2 * GM_WIDTH + 2 * D_MODEL

PEER_HEADS = 8
PEER_KEYS = 128
PEER_EXPERTS = PEER_KEYS * PEER_KEYS
PEER_DKEY = 256
PEER_HALF = PEER_DKEY // 2
PEER_TOPK = 16
PEER_TOKEN_BLOCK = 128

PLE_DIM = 256

RMS_EPS = 1e-6

kernel_name = "hybrid_rwkv7_gmlp_peer_trunk"


def _split_points():
    sizes = (RW_WIDTH, RW_WIDTH, RW_WIDTH, RW_DECAY_LORA, RW_AAA_LORA, RW_GATE_LORA,
             GM_WIDTH, GM_WIDTH, D_MODEL, D_MODEL)
    pts, acc = [], 0
    for s in sizes[:-1]:
        acc += s
        pts.append(acc)
    return tuple(pts)


def _rmsnorm(x, g):
    xf = x.astype(jnp.float32)
    y = xf * lax.rsqrt(jnp.mean(xf * xf, axis=-1, keepdims=True) + RMS_EPS)
    return (y * g.astype(jnp.float32)).astype(x.dtype)


def _layernorm(x, g, b, eps):
    xf = x.astype(jnp.float32)
    mu = jnp.mean(xf, axis=-1, keepdims=True)
    var = jnp.mean(jnp.square(xf - mu), axis=-1, keepdims=True)
    y = (xf - mu) * lax.rsqrt(var + eps) * g.astype(jnp.float32) + b.astype(jnp.float32)
    return y.astype(x.dtype)


def _token_shift(x):
    return jnp.pad(x, ((0, 0), (1, 0), (0, 0)))[:, :-1]


def _wkv7_scan(r, w, k, v, a, b):
    B, S, H, N = r.shape

    def step(state, inp):
        r_t, w_t, k_t, v_t, a_t, b_t = inp
        sa = jnp.einsum('bhvk,bhk->bhv', state, a_t)
        state = (state * w_t[:, :, None, :]
                 + sa[:, :, :, None] * b_t[:, :, None, :]
                 + v_t[:, :, :, None] * k_t[:, :, None, :])
        y_t = jnp.einsum('bhvk,bhk->bhv', state, r_t)
        return state, y_t

    xs = tuple(jnp.moveaxis(t, 1, 0) for t in (r, w, k, v, a, b))
    state0 = jnp.zeros((B, H, N, N), jnp.float32)
    _, ys = lax.scan(step, state0, xs)
    return jnp.moveaxis(ys, 0, 1)


def _rwkv7_branch(zr, zk, zv, zw, za, zg, w0, w2, a0, a2, g2, k_k, k_a, r_k, ln_g, ln_b):
    B, S, _ = zr.shape
    f32 = jnp.float32
    zw, za, zg = zw.astype(f32), za.astype(f32), zg.astype(f32)
    w_log = -jax.nn.softplus(-(w0.astype(f32) + jnp.tanh(zw) @ w2.astype(f32))) - 0.5
    decay = jnp.exp(-jnp.exp(w_log))
    a_rate = jax.nn.sigmoid(a0.astype(f32) + za @ a2.astype(f32))
    gate = jax.nn.sigmoid(zg) @ g2.astype(f32)
    kf = zk.astype(f32)
    kk = kf * k_k.astype(f32)
    k = kf * (1.0 + (a_rate - 1.0) * k_a.astype(f32))

    def heads(t):
        return t.astype(f32).reshape(B, S, RW_HEADS, RW_HEAD_DIM)

    r, k, v, kk, decay, a_rate = heads(zr), heads(k), heads(zv), heads(kk), heads(decay), heads(a_rate)
    kk = kk * lax.rsqrt(jnp.maximum(jnp.sum(kk * kk, axis=-1, keepdims=True), 1e-24))
    y = _wkv7_scan(r, decay, k, v, -kk, kk * a_rate)
    mu = jnp.mean(y, axis=-1, keepdims=True)
    var = jnp.mean(jnp.square(y - mu), axis=-1, keepdims=True)
    y = (y - mu) * lax.rsqrt(var + RW_LN_EPS)
    y = y * ln_g.astype(f32).reshape(RW_HEADS, RW_HEAD_DIM) + ln_b.astype(f32).reshape(RW_HEADS, RW_HEAD_DIM)
    y = y + jnp.sum(r * k * r_k.astype(f32), axis=-1, keepdims=True) * v
    return (y.reshape(B, S, RW_WIDTH) * gate).astype(zr.dtype)


def _gmlp_branch(zu, zv, ln_g, ln_b, ws, bs):
    B, S, _ = zu.shape
    u = jax.nn.gelu(zu)
    v = _layernorm(jax.nn.gelu(zv), ln_g, ln_b, GM_LN_EPS)
    vb = v.reshape(B, S // GM_CHUNK, GM_CHUNK, GM_GROUPS, GM_GROUP_DIM)
    mask = jnp.tril(jnp.ones((GM_CHUNK, GM_CHUNK), dtype=bool))
    w_c = jnp.where(mask[None], ws, jnp.zeros_like(ws))
    s = jnp.einsum('gtp,bnpgc->bntgc', w_c, vb) + bs.T[None, None, :, :, None]
    return u * s.reshape(B, S, GM_WIDTH)


def _peer(x, wq, keys, u_tab, v_tab):
    B, S, D = x.shape
    T = B * S
    xt = x.reshape(T // PEER_TOKEN_BLOCK, PEER_TOKEN_BLOCK, D)

    def block(xb):
        tb = xb.shape[0]
        q = (xb @ wq).reshape(tb, PEER_HEADS, 2, PEER_HALF).astype(jnp.float32)
        sc = jnp.einsum('thpc,hpkc->thpk', q, keys.astype(jnp.float32))
        s_top, i_top = lax.top_k(sc, PEER_TOPK)
        cand = s_top[:, :, 0, :, None] + s_top[:, :, 1, None, :]
        best, pos = lax.top_k(cand.reshape(tb, PEER_HEADS, PEER_TOPK * PEER_TOPK), PEER_TOPK)
        idx = (jnp.take_along_axis(i_top[:, :, 0, :], pos // PEER_TOPK, axis=-1) * PEER_KEYS
               + jnp.take_along_axis(i_top[:, :, 1, :], pos % PEER_TOPK, axis=-1))
        gw = jax.nn.softmax(best, axis=-1).astype(xb.dtype)
        u_sel = u_tab[idx]
        act = jax.nn.gelu(jnp.einsum('thkd,td->thk', u_sel, xb))
        v_sel = v_tab[idx]
        return jnp.einsum('thk,thkd->td', gw * act, v_sel)

    out = lax.map(block, xt)
    return out.reshape(B, S, D)


def setup_inputs(seed: int = 0) -> dict:
    key = jax.random.key(seed)
    ks = jax.random.split(key, 32)

    def nrm(j, shape, scale):
        return scale * jax.random.normal(ks[j], shape, jnp.float32)

    L = DEPTH
    D = D_MODEL
    return {
        "x": nrm(0, (BATCH, SEQ, D), 1.0),
        "p": nrm(1, (DEPTH, BATCH, SEQ, PLE_DIM), 1.0),
        "norm_mix": 1.0 + nrm(2, (L, D), 0.02),
        "norm_ffn": 1.0 + nrm(3, (L, D), 0.02),
        "norm_final": 1.0 + nrm(4, (D,), 0.02),
        "w_in": nrm(5, (L, 2 * D, C_IN), (2 * D) ** -0.5),
        "rw_w0": nrm(6, (L, RW_WIDTH), 0.5),
        "rw_w2": nrm(7, (L, RW_DECAY_LORA, RW_WIDTH), 0.5 * RW_DECAY_LORA ** -0.5),
        "rw_a0": nrm(8, (L, RW_WIDTH), 0.1),
        "rw_a2": nrm(9, (L, RW_AAA_LORA, RW_WIDTH), 0.5 * RW_AAA_LORA ** -0.5),
        "rw_g2": nrm(10, (L, RW_GATE_LORA, RW_WIDTH), RW_GATE_LORA ** -0.5),
        "rw_kk": 0.85 + nrm(11, (L, RW_WIDTH), 0.02),
        "rw_ka": 1.0 + nrm(12, (L, RW_WIDTH), 0.02),
        "rw_rk": nrm(13, (L, RW_HEADS, RW_HEAD_DIM), 0.1),
        "rw_ln_g": 1.0 + nrm(14, (L, RW_WIDTH), 0.02),
        "rw_ln_b": nrm(15, (L, RW_WIDTH), 0.02),
        "rw_mv_w1": nrm(16, (L - 1, 2 * D, RW_MV_LORA), (2 * D) ** -0.5),
        "rw_mv_w2": nrm(17, (L - 1, RW_MV_LORA, RW_WIDTH), RW_MV_LORA ** -0.5),
        "rw_mv_v0": nrm(18, (L - 1, RW_WIDTH), 0.1),
        "gm_ln_g": 1.0 + nrm(19, (L, GM_WIDTH), 0.02),
        "gm_ln_b": nrm(20, (L, GM_WIDTH), 0.02),
        "gm_ws": nrm(21, (L, GM_GROUPS, GM_CHUNK, GM_CHUNK), 0.5 * GM_CHUNK ** -0.5),
        "gm_bs": 1.0 + nrm(22, (L, GM_GROUPS, GM_CHUNK), 0.02),
        "w_proj_a": nrm(23, (L, RW_WIDTH, D), RW_WIDTH ** -0.5),
        "w_proj_b": nrm(24, (L, GM_WIDTH, D), GM_WIDTH ** -0.5),
        "w_out": nrm(25, (L, D, D), D ** -0.5),
        "peer_wq": nrm(26, (L, D, PEER_HEADS * PEER_DKEY), D ** -0.5),
        "peer_keys": nrm(27, (L, PEER_HEADS, 2, PEER_KEYS, PEER_HALF), PEER_HALF ** -0.5),
        "peer_u": nrm(28, (L, PEER_EXPERTS, D), D ** -0.5),
        "peer_v": nrm(29, (L, PEER_EXPERTS, D), 0.1),
        "ple_w": nrm(30, (L, PLE_DIM, D), 0.5 * PLE_DIM ** -0.5),
        "ple_gate": nrm(31, (L, D, D), D ** -0.5),
    }


def reference(x, p, norm_mix, norm_ffn, norm_final, w_in, rw_w0, rw_w2, rw_a0, rw_a2, rw_g2,
              rw_kk, rw_ka, rw_rk, rw_ln_g, rw_ln_b, rw_mv_w1, rw_mv_w2, rw_mv_v0,
              gm_ln_g, gm_ln_b, gm_ws, gm_bs, w_proj_a, w_proj_b, w_out,
              peer_wq, peer_keys, peer_u, peer_v, ple_w, ple_gate):
    split_pts = _split_points()
    h = x
    v_first = None
    for i in range(DEPTH):
        n = _rmsnorm(h, norm_mix[i])
        nx = jnp.concatenate([n, _token_shift(n)], axis=-1)
        z = nx @ w_in[i]
        zr, zk, zv, zw, za, zg, zu, zgv, zga, zgb = jnp.split(z, split_pts, axis=-1)
        if i == 0:
            v_first = zv
        else:
            vmix = jax.nn.sigmoid(rw_mv_v0[i - 1] + (nx @ rw_mv_w1[i - 1]) @ rw_mv_w2[i - 1])
            zv = zv + (v_first - zv) * vmix
        ya = _rwkv7_branch(zr, zk, zv, zw, za, zg, rw_w0[i], rw_w2[i], rw_a0[i], rw_a2[i], rw_g2[i],
                           rw_kk[i], rw_ka[i], rw_rk[i], rw_ln_g[i], rw_ln_b[i])
        yb = _gmlp_branch(zu, zgv, gm_ln_g[i], gm_ln_b[i], gm_ws[i], gm_bs[i])
        merged = (jax.nn.sigmoid(zga) * (ya @ w_proj_a[i])
                  + jax.nn.sigmoid(zgb) * (yb @ w_proj_b[i]))
        h = h + merged @ w_out[i]
        n2 = _rmsnorm(h, norm_ffn[i])
        h = h + _peer(n2, peer_wq[i], peer_keys[i], peer_u[i], peer_v[i])
        h = h + (p[i] @ ple_w[i]) * jax.nn.sigmoid(n2 @ ple_gate[i])
    return _rmsnorm(h, norm_final)
```

```python
import functools
import math

import jax
import jax.numpy as jnp
from jax import lax
from jax.experimental import pallas as pl
from jax.experimental.pallas import tpu as pltpu

F32 = jnp.float32
BF16 = jnp.bfloat16

RW_HEADS = 16
RW_HEAD_DIM = 64
RW_LN_EPS = 64e-5
GM_GROUPS = 8
GM_CHUNK = 128
GM_LN_EPS = 1e-5
PEER_HEADS = 8
PEER_KEYS = 128
PEER_HALF = 128
PEER_TOPK = 16
RMS_EPS = 1e-6

LANES = 128
SUBLANES = 8
VMEM_LIMIT = 56 * 1024 * 1024


def _cparams(*sem):
    return pltpu.CompilerParams(dimension_semantics=tuple(sem), vmem_limit_bytes=VMEM_LIMIT)


def _const_spec(shape):
    nd = len(shape)
    return pl.BlockSpec(shape, lambda *_: (0,) * nd, pipeline_mode=pl.Buffered(1))


def _row_spec(rows, cols):
    return pl.BlockSpec((rows, cols), lambda i: (i, 0))


def _gelu(x):
    c = math.sqrt(2.0 / math.pi)
    return 0.5 * x * (1.0 + jnp.tanh(c * (x + 0.044715 * (x * x * x))))


def _sigmoid(x):
    return 1.0 / (1.0 + jnp.exp(-x))


def _dot(a, b):
    return jnp.dot(a, b, preferred_element_type=F32)


def _rms(x, g):
    return x * lax.rsqrt(jnp.mean(x * x, axis=-1, keepdims=True) + RMS_EPS) * g


def _norm_shift_kernel(h_ref, hprev_ref, g_ref, nx_ref, *, batch):
    i = pl.program_id(0)
    g = g_ref[...]
    n = _rms(h_ref[...], g)
    prev = _rms(hprev_ref[...], g) * (i > 0).astype(F32)
    d = n.shape[1]
    shifted = jnp.concatenate([prev, n[:-batch]], axis=0)
    nx_ref[:, :d] = n.astype(BF16)
    nx_ref[:, d:] = shifted.astype(BF16)


def _norm_shift(h, g, batch, tm):
    t, d = h.shape
    per = tm // batch
    return pl.pallas_call(
        functools.partial(_norm_shift_kernel, batch=batch),
        grid=(t // tm,),
        in_specs=[_row_spec(tm, d),
                  pl.BlockSpec((batch, d), lambda i: (jnp.maximum(i * per - 1, 0), 0)),
                  _const_spec((1, d))],
        out_specs=_row_spec(tm, 2 * d),
        out_shape=jax.ShapeDtypeStruct((t, 2 * d), BF16),
        compiler_params=_cparams("parallel"),
        name="norm_shift",
    )(h, h, g.reshape(1, d))


def _rk_kernel(nx_ref, wr_ref, wk_ref, wa_ref, a2_ref, a0_ref, kk_ref, ka_ref,
               r_ref, k_ref, kkraw_ref, ar_ref):
    nx = nx_ref[...]
    r_ref[...] = _dot(nx, wr_ref[...])
    zk = _dot(nx, wk_ref[...])
    za = _dot(nx, wa_ref[...])
    a_rate = _sigmoid(a0_ref[...] + _dot(za.astype(BF16), a2_ref[...]))
    ar_ref[...] = a_rate
    kkraw_ref[...] = zk * kk_ref[...]
    k_ref[...] = zk * (1.0 + (a_rate - 1.0) * ka_ref[...])


def _rk_call(nx, wr, wk, wa, a2, a0, kk, ka, tm):
    t, d2 = nx.shape
    w = wr.shape[1]
    la = wa.shape[1]
    out = jax.ShapeDtypeStruct((t, w), F32)
    return pl.pallas_call(
        _rk_kernel,
        grid=(t // tm,),
        in_specs=[_row_spec(tm, d2), _const_spec((d2, w)), _const_spec((d2, w)),
                  _const_spec((d2, la)), _const_spec((la, w)),
                  _const_spec((1, w)), _const_spec((1, w)), _const_spec((1, w))],
        out_specs=[_row_spec(tm, w)] * 4,
        out_shape=[out] * 4,
        compiler_params=_cparams("parallel"),
        name="rwkv_rk",
    )(nx, wr, wk, wa, a2, a0, kk, ka)


def _decay_from(nx, ww_ref, w2_ref, w0_ref):
    zw = _dot(nx, ww_ref[...])
    x = -(w0_ref[...] + _dot(jnp.tanh(zw).astype(BF16), w2_ref[...]))
    softplus = jnp.maximum(x, 0.0) + jnp.log(1.0 + jnp.exp(-jnp.abs(x)))
    return jnp.exp(-jnp.exp(-softplus - 0.5))


def _vw0_kernel(nx_ref, wv_ref, ww_ref, w2_ref, w0_ref, v_ref, dec_ref):
    nx = nx_ref[...]
    v_ref[...] = _dot(nx, wv_ref[...])
    dec_ref[...] = _decay_from(nx, ww_ref, w2_ref, w0_ref)


def _vw_kernel(nx_ref, wv_ref, ww_ref, w2_ref, w0_ref, m1_ref, m2_ref, v0_ref, vfirst_ref,
               v_ref, dec_ref):
    nx = nx_ref[...]
    zv = _dot(nx, wv_ref[...])
    lo = _dot(nx, m1_ref[...])
    vmix = _sigmoid(v0_ref[...] + _dot(lo.astype(BF16), m2_ref[...]))
    v_ref[...] = zv + (vfirst_ref[...] - zv) * vmix
    dec_ref[...] = _decay_from(nx, ww_ref, w2_ref, w0_ref)


def _vw_call(nx, wv, ww, w2, w0, mv, vfirst, tm):
    t, d2 = nx.shape
    w = wv.shape[1]
    lw = ww.shape[1]
    out = jax.ShapeDtypeStruct((t, w), F32)
    in_specs = [_row_spec(tm, d2), _const_spec((d2, w)), _const_spec((d2, lw)),
                _const_spec((lw, w)), _const_spec((1, w))]
    args = [nx, wv, ww, w2, w0]
    if mv is None:
        body = _vw0_kernel
    else:
        m1, m2, v0 = mv
        lm = m1.shape[1]
        body = _vw_kernel
        in_specs += [_const_spec((d2, lm)), _const_spec((lm, w)), _const_spec((1, w)),
                     _row_spec(tm, w)]
        args += [m1, m2, v0, vfirst]
    return pl.pallas_call(
        body,
        grid=(t // tm,),
        in_specs=in_specs,
        out_specs=[_row_spec(tm, w)] * 2,
        out_shape=[out] * 2,
        compiler_params=_cparams("parallel"),
        name="rwkv_vw",
    )(*args)


def _to_lanes(blk):
    r = jnp.concatenate([blk[s * 8:(s + 1) * 8, j * LANES:(j + 1) * LANES]
                         for s in range(2) for j in range(8)], axis=0)
    tr = r.T
    lo, hi = tr[0:64], tr[64:128]
    lane = lax.broadcasted_iota(jnp.int32, (64, LANES), 1)
    first = lane < 64
    return (jnp.where(first, lo, pltpu.roll(hi, 64, 1)),
            jnp.where(first, pltpu.roll(lo, 64, 1), hi))


def _from_lanes(o0, o1):
    lane = lax.broadcasted_iota(jnp.int32, (64, LANES), 1)
    first = lane < 64
    lo = jnp.where(first, o0, pltpu.roll(o1, 64, 1))
    hi = jnp.where(first, pltpu.roll(o0, 64, 1), o1)
    r = jnp.concatenate([lo, hi], axis=0).T
    rows = [jnp.concatenate([r[s * 64 + j * 8:s * 64 + j * 8 + 8] for j in range(8)], axis=1)
            for s in range(2)]
    return jnp.concatenate(rows, axis=0)


def _wkv_kernel(r_ref, w_ref, k_ref, v_ref, kk_ref, ar_ref, rk_ref, lng_ref, lnb_ref,
                y_ref, st_ref, rt, wt, kt, vt, at, bt, yt, *, steps):
    n = RW_HEAD_DIM

    @pl.when(pl.program_id(0) == 0)
    def _():
        st_ref[...] = jnp.zeros_like(st_ref)

    def load_pair(sp, _):
        rows = pl.ds(pl.multiple_of(sp * 16, 16), 16)
        for src, dst in ((r_ref, rt), (w_ref, wt), (k_ref, kt), (v_ref, vt)):
            o0, o1 = _to_lanes(src[rows, :])
            dst[2 * sp] = o0
            dst[2 * sp + 1] = o1
        kk0, kk1 = _to_lanes(kk_ref[rows, :])
        ar0, ar1 = _to_lanes(ar_ref[rows, :])
        for off, kk, ar in ((0, kk0, ar0), (1, kk1, ar1)):
            ss = jnp.sum(kk * kk, axis=0, keepdims=True)
            kkn = kk * lax.rsqrt(jnp.maximum(ss, 1e-24))
            at[2 * sp + off] = -kkn
            bt[2 * sp + off] = kkn * ar
        return 0

    lax.fori_loop(0, steps // 2, load_pair, 0)

    def step(s, _):
        sa = jnp.zeros((n, LANES), F32)
        for k in range(n):
            sa = sa + st_ref[k] * at[s, k:k + 1, :]
        vv = vt[s]
        y = jnp.zeros((n, LANES), F32)
        for k in range(n):
            sk = (st_ref[k] * wt[s, k:k + 1, :] + sa * bt[s, k:k + 1, :]
                  + vv * kt[s, k:k + 1, :])
            st_ref[k] = sk
            y = y + sk * rt[s, k:k + 1, :]
        yt[s] = y
        return 0

    lax.fori_loop(0, steps, step, 0)

    rk = rk_ref[...]
    lng = lng_ref[...]
    lnb = lnb_ref[...]

    def finish(s):
        y = yt[s]
        mu = jnp.mean(y, axis=0, keepdims=True)
        yc = y - mu
        var = jnp.mean(yc * yc, axis=0, keepdims=True)
        yn = yc * lax.rsqrt(var + RW_LN_EPS) * lng + lnb
        bonus = jnp.sum(rt[s] * kt[s] * rk, axis=0, keepdims=True)
        return yn + bonus * vt[s]

    def store_pair(sp, _):
        rows = pl.ds(pl.multiple_of(sp * 16, 16), 16)
        y_ref[rows, :] = _from_lanes(finish(2 * sp), finish(2 * sp + 1))
        return 0

    lax.fori_loop(0, steps // 2, store_pair, 0)


def _wkv_call(r, w, k, v, kkraw, ar, rk_l, lng_l, lnb_l, batch, steps):
    t, width = r.shape
    rows = steps * batch
    n = RW_HEAD_DIM
    slab = pltpu.VMEM((steps, n, LANES), F32)
    return pl.pallas_call(
        functools.partial(_wkv_kernel, steps=steps),
        grid=(t // rows,),
        in_specs=[_row_spec(rows, width)] * 6 + [_const_spec((n, LANES))] * 3,
        out_specs=_row_spec(rows, width),
        out_shape=jax.ShapeDtypeStruct((t, width), F32),
        scratch_shapes=[pltpu.VMEM((n, n, LANES), F32)] + [slab] * 7,
        compiler_params=_cparams("arbitrary"),
        name="wkv7_scan",
    )(r, w, k, v, kkraw, ar, rk_l, lng_l, lnb_l)


def _head_param_lanes(pv, batch):
    a = pv.reshape(RW_HEADS // 2, 2, RW_HEAD_DIM).transpose(2, 1, 0)
    a = jnp.broadcast_to(a[..., None], a.shape + (batch,))
    return a.reshape(RW_HEAD_DIM, LANES).astype(F32)


def _gmlp_kernel(nx_ref, wu_ref, wg_ref, lng_ref, lnb_ref, ws_ref, bias_ref, yb_ref):
    nx = nx_ref[...]
    u = _gelu(_dot(nx, wu_ref[...]))
    gv = _gelu(_dot(nx, wg_ref[...]))
    mu = jnp.mean(gv, axis=-1, keepdims=True)
    gc = gv - mu
    var = jnp.mean(gc * gc, axis=-1, keepdims=True)
    v = (gc * lax.rsqrt(var + GM_LN_EPS) * lng_ref[...] + lnb_ref[...]).astype(BF16)
    gd = v.shape[1] // GM_GROUPS
    for g in range(GM_GROUPS):
        cols = slice(g * gd, (g + 1) * gd)
        s = _dot(ws_ref[g], v[:, cols]) + bias_ref[:, g:g + 1]
        yb_ref[:, cols] = (u[:, cols] * s).astype(BF16)


def _gmlp_call(nx, wu, wg, lng, lnb, ws_k, bias):
    t, d2 = nx.shape
    w = wu.shape[1]
    rows = ws_k.shape[1]
    return pl.pallas_call(
        _gmlp_kernel,
        grid=(t // rows,),
        in_specs=[_row_spec(rows, d2), _const_spec((d2, w)), _const_spec((d2, w)),
                  _const_spec((1, w)), _const_spec((1, w)),
                  _const_spec(ws_k.shape), _const_spec((rows, GM_GROUPS))],
        out_specs=_row_spec(rows, w),
        out_shape=jax.ShapeDtypeStruct((t, w), BF16),
        compiler_params=_cparams("parallel"),
        name="gmlp",
    )(nx, wu, wg, lng, lnb, ws_k, bias)


def _merge_kernel(nx_ref, y_ref, yb_ref, h_ref, wg_ref, g2_ref, wga_ref, wgb_ref,
                  wa_ref, wb_ref, wo_ref, nf_ref, hn_ref, n2_ref):
    nx = nx_ref[...]
    gate = _dot(_sigmoid(_dot(nx, wg_ref[...])).astype(BF16), g2_ref[...])
    ya = (y_ref[...] * gate).astype(BF16)
    pa = _dot(ya, wa_ref[...])
    pb = _dot(yb_ref[...], wb_ref[...])
    merged = _sigmoid(_dot(nx, wga_ref[...])) * pa + _sigmoid(_dot(nx, wgb_ref[...])) * pb
    hn = h_ref[...] + _dot(merged.astype(BF16), wo_ref[...])
    hn_ref[...] = hn
    n2_ref[...] = _rms(hn, nf_ref[...]).astype(BF16)


def _merge_call(nx, y, yb, h, wg, g2, wga, wgb, wa, wb, wo, nf, tm):
    t, d2 = nx.shape
    d = h.shape[1]
    w = y.shape[1]
    lg = wg.shape[1]
    return pl.pallas_call(
        _merge_kernel,
        grid=(t // tm,),
        in_specs=[_row_spec(tm, d2), _row_spec(tm, w), _row_spec(tm, w), _row_spec(tm, d),
                  _const_spec((d2, lg)), _const_spec((lg, w)),
                  _const_spec((d2, d)), _const_spec((d2, d)),
                  _const_spec((w, d)), _const_spec((w, d)), _const_spec((d, d)),
                  _const_spec((1, d))],
        out_specs=[_row_spec(tm, d), _row_spec(tm, d)],
        out_shape=[jax.ShapeDtypeStruct((t, d), F32), jax.ShapeDtypeStruct((t, d), BF16)],
        compiler_params=_cparams("parallel"),
        name="merge",
    )(nx, y, yb, h, wg, g2, wga, wgb, wa, wb, wo, nf)


def _top_values(s, count):
    tops = []
    for _ in range(count):
        m = jnp.max(s, axis=0, keepdims=True)
        tops.append(m)
        s = jnp.where(s == m, -jnp.inf, s)
    return jnp.concatenate(tops, axis=0)


def _peer_select_kernel(n2_ref, h_ref, p_ref, wq_ref, keys_ref, wple_ref, wpg_ref,
                        s1_ref, s2_ref, e1_ref, e2_ref, tau_ref, hp_ref, q_scr):
    n2 = n2_ref[...]
    ple = _dot(p_ref[...], wple_ref[...]) * _sigmoid(_dot(n2, wpg_ref[...]))
    hp_ref[...] = h_ref[...] + ple
    q = _dot(n2, wq_ref[...]).astype(BF16)
    for hp in range(2 * PEER_HEADS):
        q_scr[hp] = q[:, hp * PEER_HALF:(hp + 1) * PEER_HALF]
    tt = n2.shape[0]
    kk = PEER_TOPK
    nt = (((1,), (1,)), ((), ()))

    def head(h, _):
        s1 = lax.dot_general(keys_ref[h, 0], q_scr[2 * h], nt, preferred_element_type=F32)
        s2 = lax.dot_general(keys_ref[h, 1], q_scr[2 * h + 1], nt, preferred_element_type=F32)
        s1_ref[h] = s1
        s2_ref[h] = s2
        for c in range(tt // LANES):
            cols = slice(c * LANES, (c + 1) * LANES)
            t1 = _top_values(s1[:, cols], kk)
            t2 = _top_values(s2[:, cols], kk)
            cand = [t1[0:1] + t2]
            cand += [t1[i:i + 1] + t2[0:8] for i in range(1, 8)]
            cand += [t1[8:16] + t2[0:1]]
            best = _top_values(jnp.concatenate(cand, axis=0), kk)
            inv_z = 1.0 / jnp.sum(jnp.exp(best - best[0:1]), axis=0, keepdims=True)
            tau_ref[h, :, cols] = best[kk - 1:kk]
            e1_ref[h, :, cols] = jnp.exp(s1[:, cols] - t1[0:1])
            e2_ref[h, :, cols] = jnp.exp(s2[:, cols] - t2[0:1]) * inv_z
        return 0

    lax.fori_loop(0, PEER_HEADS, head, 0)


def _peer_select_call(n2, h, p, wq, keys, wple, wpg, tt):
    t, d = n2.shape
    pd = p.shape[1]
    dq = wq.shape[1]
    nh, nk = PEER_HEADS, PEER_KEYS
    tspec = pl.BlockSpec((nh, nk, tt), lambda i: (0, 0, i))
    tshape = jax.ShapeDtypeStruct((nh, nk, t), F32)
    return pl.pallas_call(
        _peer_select_kernel,
        grid=(t // tt,),
        in_specs=[_row_spec(tt, d), _row_spec(tt, d), _row_spec(tt, pd),
                  _const_spec((d, dq)), _const_spec(keys.shape),
                  _const_spec((pd, d)), _const_spec((d, d))],
        out_specs=[tspec] * 4 + [pl.BlockSpec((nh, 1, tt), lambda i: (0, 0, i)), _row_spec(tt, d)],
        out_shape=[tshape] * 4 + [jax.ShapeDtypeStruct((nh, 1, t), F32),
                                  jax.ShapeDtypeStruct((t, d), F32)],
        scratch_shapes=[pltpu.VMEM((2 * nh, tt, PEER_HALF), BF16)],
        compiler_params=_cparams("parallel"),
        name="peer_select",
    )(n2, h, p, wq, keys, wple, wpg)


def _peer_dense_kernel(n2_ref, u_ref, vt_ref, s1_ref, e1_ref, s2_ref, e2_ref, tau_ref, hp_ref,
                       out_ref, acc_ref, act_ref, ga_ref):
    j = pl.program_id(1)

    @pl.when(j == 0)
    def _():
        acc_ref[...] = jnp.zeros_like(acc_ref)

    nt = (((1,), (1,)), ((), ()))
    act_ref[...] = _gelu(lax.dot_general(u_ref[...], n2_ref[...], nt, preferred_element_type=F32))
    tt = n2_ref.shape[0]
    nk = PEER_KEYS

    for i1 in range(u_ref.shape[0] // nk):
        rows = slice(i1 * nk, (i1 + 1) * nk)
        for c in range(tt // LANES):
            cols = slice(c * LANES, (c + 1) * LANES)
            g = jnp.zeros((nk, LANES), F32)
            for h in range(PEER_HEADS):
                tsum = s1_ref[h, i1:i1 + 1, cols] + s2_ref[h, :, cols]
                val = e1_ref[h, i1:i1 + 1, cols] * e2_ref[h, :, cols]
                g = g + jnp.where(tsum >= tau_ref[h, :, cols], val, 0.0)
            ga_ref[rows, cols] = (g * act_ref[rows, cols]).astype(BF16)
    acc_ref[...] += _dot(vt_ref[...], ga_ref[...])

    @pl.when(j == pl.num_programs(1) - 1)
    def _():
        out_ref[...] = hp_ref[...] + acc_ref[...].T


def _peer_dense_call(n2, u, vt, s1, e1, s2, e2, tau, hp, tt, te):
    t, d = n2.shape
    ne = u.shape[0]
    nh, nk = PEER_HEADS, PEER_KEYS
    r1 = te // nk
    return pl.pallas_call(
        _peer_dense_kernel,
        grid=(t // tt, ne // te),
        in_specs=[pl.BlockSpec((tt, d), lambda i, j: (i, 0)),
                  pl.BlockSpec((te, d), lambda i, j: (j, 0)),
                  pl.BlockSpec((d, te), lambda i, j: (0, j)),
                  pl.BlockSpec((nh, r1, tt), lambda i, j: (0, j, i)),
                  pl.BlockSpec((nh, r1, tt), lambda i, j: (0, j, i)),
                  pl.BlockSpec((nh, nk, tt), lambda i, j: (0, 0, i)),
                  pl.BlockSpec((nh, nk, tt), lambda i, j: (0, 0, i)),
                  pl.BlockSpec((nh, 1, tt), lambda i, j: (0, 0, i)),
                  pl.BlockSpec((tt, d), lambda i, j: (i, 0))],
        out_specs=pl.BlockSpec((tt, d), lambda i, j: (i, 0)),
        out_shape=jax.ShapeDtypeStruct((t, d), F32),
        scratch_shapes=[pltpu.VMEM((d, tt), F32), pltpu.VMEM((te, tt), F32),
                        pltpu.VMEM((te, tt), BF16)],
        compiler_params=_cparams("parallel", "arbitrary"),
        name="peer_dense",
    )(n2, u, vt, s1, e1, s2, e2, tau, hp)


def _final_norm_kernel(h_ref, g_ref, o_ref):
    o_ref[0] = _rms(h_ref[...], g_ref[...])


def _final_norm_call(h2d, g, batch, seq, ts):
    d = g.shape[-1]
    return pl.pallas_call(
        _final_norm_kernel,
        grid=(seq // ts, batch),
        in_specs=[pl.BlockSpec((ts, d), lambda s, b: (s, b)), _const_spec((1, d))],
        out_specs=pl.BlockSpec((1, ts, d), lambda s, b: (b, s, 0)),
        out_shape=jax.ShapeDtypeStruct((batch, seq, d), F32),
        compiler_params=_cparams("parallel", "parallel"),
        name="final_norm",
    )(h2d, g.reshape(1, d))


def _tile(total, want):
    want = min(want, total)
    assert total % want == 0, (total, want)
    return want


def kernel(x, p, norm_mix, norm_ffn, norm_final, w_in, rw_w0, rw_w2, rw_a0, rw_a2, rw_g2, rw_kk, rw_ka, rw_rk, rw_ln_g, rw_ln_b, rw_mv_w1, rw_mv_w2, rw_mv_v0, gm_ln_g, gm_ln_b, gm_ws, gm_bs, w_proj_a, w_proj_b, w_out, peer_wq, peer_keys, peer_u, peer_v, ple_w, ple_gate):
    batch, seq, d = x.shape
    depth = w_in.shape[0]
    t = batch * seq
    w = RW_HEADS * RW_HEAD_DIM
    assert batch * RW_HEADS == LANES and batch == SUBLANES and d == w
    assert seq % GM_CHUNK == 0

    tm = _tile(t, 512)
    tt = _tile(t, 512)
    te = 1024
    steps = _tile(seq, 32)

    gw = gm_ln_g.shape[1]
    sizes = (w, w, w, rw_w2.shape[1], rw_a2.shape[1], rw_g2.shape[1], gw, gw, d, d)
    offs = [0]
    for sz in sizes:
        offs.append(offs[-1] + sz)

    def col(wi, idx):
        return wi[:, offs[idx]:offs[idx + 1]].astype(BF16)

    h = x.transpose(1, 0, 2).reshape(t, d)
    pt = p.transpose(0, 2, 1, 3).reshape(depth, t, p.shape[-1]).astype(BF16)
    row = lambda a: a.reshape(1, -1).astype(F32)
    eye = jnp.eye(batch, dtype=F32)
    tril = jnp.tril(jnp.ones((GM_CHUNK, GM_CHUNK), F32))

    v_first = None
    for i in range(depth):
        wi = w_in[i]
        nx = _norm_shift(h, norm_mix[i], batch, tm)

        r, k, kkraw, ar = _rk_call(nx, col(wi, 0), col(wi, 1), col(wi, 4), rw_a2[i].astype(BF16),
                                   row(rw_a0[i]), row(rw_kk[i]), row(rw_ka[i]), tm)
        mv = None
        if i > 0:
            mv = (rw_mv_w1[i - 1].astype(BF16), rw_mv_w2[i - 1].astype(BF16), row(rw_mv_v0[i - 1]))
        v, dec = _vw_call(nx, col(wi, 2), col(wi, 3), rw_w2[i].astype(BF16), row(rw_w0[i]), mv,
                          v_first, tm)
        if i == 0:
            v_first = v

        y = _wkv_call(r, dec, k, v, kkraw, ar,
                      _head_param_lanes(rw_rk[i], batch),
                      _head_param_lanes(rw_ln_g[i].reshape(RW_HEADS, RW_HEAD_DIM), batch),
                      _head_param_lanes(rw_ln_b[i].reshape(RW_HEADS, RW_HEAD_DIM), batch),
                      batch, steps)

        ws_k = jax.vmap(lambda m: jnp.kron(m * tril, eye))(gm_ws[i]).astype(BF16)
        bias = jnp.repeat(gm_bs[i].T, batch, axis=0).astype(F32)
        yb = _gmlp_call(nx, col(wi, 6), col(wi, 7), row(gm_ln_g[i]), row(gm_ln_b[i]), ws_k, bias)

        h, n2 = _merge_call(nx, y, yb, h, col(wi, 5), rw_g2[i].astype(BF16), col(wi, 8), col(wi, 9),
                            w_proj_a[i].astype(BF16), w_proj_b[i].astype(BF16),
                            w_out[i].astype(BF16), row(norm_ffn[i]), tm)

        s1, s2, e1, e2, tau, hp = _peer_select_call(
            n2, h, pt[i], peer_wq[i].astype(BF16), peer_keys[i].astype(BF16),
            ple_w[i].astype(BF16), ple_gate[i].astype(BF16), tt)
        h = _peer_dense_call(n2, peer_u[i].astype(BF16), peer_v[i].T.astype(BF16),
                             s1, e1, s2, e2, tau, hp, tt, te)

    return _final_norm_call(h.reshape(seq, batch * d), norm_final, batch, seq, _tile(seq, 512))
```

```python
import functools
import math

import jax
import jax.numpy as jnp
from jax import lax
from jax.experimental import pallas as pl
from jax.experimental.pallas import tpu as pltpu

F32 = jnp.float32
BF16 = jnp.bfloat16

RW_HEADS = 16
RW_HEAD_DIM = 64
RW_LN_EPS = 64e-5
GM_GROUPS = 8
GM_CHUNK = 128
GM_LN_EPS = 1e-5
PEER_HEADS = 8
PEER_KEYS = 128
PEER_HALF = 128
PEER_TOPK = 16
PEER_CHUNK = 256
RMS_EPS = 1e-6

LANES = 128
SUBLANES = 8
VMEM_LIMIT = 56 * 1024 * 1024


def _cparams(*sem):
    return pltpu.CompilerParams(dimension_semantics=tuple(sem), vmem_limit_bytes=VMEM_LIMIT)


def _const_spec(shape):
    nd = len(shape)
    return pl.BlockSpec(shape, lambda *_: (0,) * nd, pipeline_mode=pl.Buffered(1))


def _row_spec(rows, cols):
    return pl.BlockSpec((rows, cols), lambda i: (i, 0))


def _gelu(x):
    c = math.sqrt(2.0 / math.pi)
    return 0.5 * x * (1.0 + jnp.tanh(c * (x + 0.044715 * (x * x * x))))


def _sigmoid(x):
    return 1.0 / (1.0 + jnp.exp(-x))


def _dot(a, b):
    return jnp.dot(a, b, preferred_element_type=F32)


def _rms(x, g):
    return x * lax.rsqrt(jnp.mean(x * x, axis=-1, keepdims=True) + RMS_EPS) * g


def _norm_shift_kernel(h_ref, hprev_ref, g_ref, nx_ref, *, batch):
    i = pl.program_id(0)
    g = g_ref[...]
    n = _rms(h_ref[...], g)
    prev = _rms(hprev_ref[...], g) * (i > 0).astype(F32)
    d = n.shape[1]
    shifted = jnp.concatenate([prev, n[:-batch]], axis=0)
    nx_ref[:, :d] = n.astype(BF16)
    nx_ref[:, d:] = shifted.astype(BF16)


def _norm_shift(h, g, batch, tm):
    t, d = h.shape
    per = tm // batch
    return pl.pallas_call(
        functools.partial(_norm_shift_kernel, batch=batch),
        grid=(t // tm,),
        in_specs=[_row_spec(tm, d),
                  pl.BlockSpec((batch, d), lambda i: (jnp.maximum(i * per - 1, 0), 0)),
                  _const_spec((1, d))],
        out_specs=_row_spec(tm, 2 * d),
        out_shape=jax.ShapeDtypeStruct((t, 2 * d), BF16),
        compiler_params=_cparams("parallel"),
        name="norm_shift",
    )(h, h, g.reshape(1, d))


def _rk_kernel(nx_ref, wr_ref, wk_ref, wa_ref, a2_ref, a0_ref, kk_ref, ka_ref,
               r_ref, k_ref, kkraw_ref, ar_ref):
    nx = nx_ref[...]
    r_ref[...] = _dot(nx, wr_ref[...])
    zk = _dot(nx, wk_ref[...])
    za = _dot(nx, wa_ref[...])
    a_rate = _sigmoid(a0_ref[...] + _dot(za.astype(BF16), a2_ref[...]))
    ar_ref[...] = a_rate
    kkraw_ref[...] = zk * kk_ref[...]
    k_ref[...] = zk * (1.0 + (a_rate - 1.0) * ka_ref[...])


def _rk_call(nx, wr, wk, wa, a2, a0, kk, ka, tm):
    t, d2 = nx.shape
    w = wr.shape[1]
    la = wa.shape[1]
    out = jax.ShapeDtypeStruct((t, w), F32)
    return pl.pallas_call(
        _rk_kernel,
        grid=(t // tm,),
        in_specs=[_row_spec(tm, d2), _const_spec((d2, w)), _const_spec((d2, w)),
                  _const_spec((d2, la)), _const_spec((la, w)),
                  _const_spec((1, w)), _const_spec((1, w)), _const_spec((1, w))],
        out_specs=[_row_spec(tm, w)] * 4,
        out_shape=[out] * 4,
        compiler_params=_cparams("parallel"),
        name="rwkv_rk",
    )(nx, wr, wk, wa, a2, a0, kk, ka)


def _decay_from(nx, ww_ref, w2_ref, w0_ref):
    zw = _dot(nx, ww_ref[...])
    x = -(w0_ref[...] + _dot(jnp.tanh(zw).astype(BF16), w2_ref[...]))
    softplus = jnp.maximum(x, 0.0) + jnp.log(1.0 + jnp.exp(-jnp.abs(x)))
    return jnp.exp(-jnp.exp(-softplus - 0.5))


def _vw0_kernel(nx_ref, wv_ref, ww_ref, w2_ref, w0_ref, v_ref, dec_ref):
    nx = nx_ref[...]
    v_ref[...] = _dot(nx, wv_ref[...])
    dec_ref[...] = _decay_from(nx, ww_ref, w2_ref, w0_ref)


def _vw_kernel(nx_ref, wv_ref, ww_ref, w2_ref, w0_ref, m1_ref, m2_ref, v0_ref, vfirst_ref,
               v_ref, dec_ref):
    nx = nx_ref[...]
    zv = _dot(nx, wv_ref[...])
    lo = _dot(nx, m1_ref[...])
    vmix = _sigmoid(v0_ref[...] + _dot(lo.astype(BF16), m2_ref[...]))
    v_ref[...] = zv + (vfirst_ref[...] - zv) * vmix
    dec_ref[...] = _decay_from(nx, ww_ref, w2_ref, w0_ref)


def _vw_call(nx, wv, ww, w2, w0, mv, vfirst, tm):
    t, d2 = nx.shape
    w = wv.shape[1]
    lw = ww.shape[1]
    out = jax.ShapeDtypeStruct((t, w), F32)
    in_specs = [_row_spec(tm, d2), _const_spec((d2, w)), _const_spec((d2, lw)),
                _const_spec((lw, w)), _const_spec((1, w))]
    args = [nx, wv, ww, w2, w0]
    if mv is None:
        body = _vw0_kernel
    else:
        m1, m2, v0 = mv
        lm = m1.shape[1]
        body = _vw_kernel
        in_specs += [_const_spec((d2, lm)), _const_spec((lm, w)), _const_spec((1, w)),
                     _row_spec(tm, w)]
        args += [m1, m2, v0, vfirst]
    return pl.pallas_call(
        body,
        grid=(t // tm,),
        in_specs=in_specs,
        out_specs=[_row_spec(tm, w)] * 2,
        out_shape=[out] * 2,
        compiler_params=_cparams("parallel"),
        name="rwkv_vw",
    )(*args)


def _to_lanes(blk):
    r = jnp.concatenate([blk[s * 8:(s + 1) * 8, j * LANES:(j + 1) * LANES]
                         for s in range(2) for j in range(8)], axis=0)
    tr = r.T
    lo, hi = tr[0:64], tr[64:128]
    lane = lax.broadcasted_iota(jnp.int32, (64, LANES), 1)
    first = lane < 64
    return (jnp.where(first, lo, pltpu.roll(hi, 64, 1)),
            jnp.where(first, pltpu.roll(lo, 64, 1), hi))


def _from_lanes(o0, o1):
    lane = lax.broadcasted_iota(jnp.int32, (64, LANES), 1)
    first = lane < 64
    lo = jnp.where(first, o0, pltpu.roll(o1, 64, 1))
    hi = jnp.where(first, pltpu.roll(o0, 64, 1), o1)
    r = jnp.concatenate([lo, hi], axis=0).T
    rows = [jnp.concatenate([r[s * 64 + j * 8:s * 64 + j * 8 + 8] for j in range(8)], axis=1)
            for s in range(2)]
    return jnp.concatenate(rows, axis=0)


_WR, _WW, _WK, _WV, _WA, _WB = range(6)


def _wkv_kernel(r_ref, w_ref, k_ref, v_ref, kk_ref, ar_ref, rk_ref, lng_ref, lnb_ref,
                y_ref, st_ref, buf_a, buf_b, y_a, y_b, *, steps):
    n = RW_HEAD_DIM
    pairs = steps // 2

    @pl.when(pl.program_id(0) == 0)
    def _():
        st_ref[...] = jnp.zeros_like(st_ref)

    def load_pair(sp, buf):
        rows = pl.ds(pl.multiple_of(sp * 16, 16), 16)
        for idx, src in ((_WR, r_ref), (_WW, w_ref), (_WK, k_ref), (_WV, v_ref)):
            o0, o1 = _to_lanes(src[rows, :])
            buf[idx, 0] = o0
            buf[idx, 1] = o1
        kk0, kk1 = _to_lanes(kk_ref[rows, :])
        ar0, ar1 = _to_lanes(ar_ref[rows, :])
        for off, kk, ar in ((0, kk0, ar0), (1, kk1, ar1)):
            ss = jnp.sum(kk * kk, axis=0, keepdims=True)
            kkn = kk * lax.rsqrt(jnp.maximum(ss, 1e-24))
            buf[_WA, off] = -kkn
            buf[_WB, off] = kkn * ar

    def step(buf, off, nbuf, noff, ybuf, sa):
        nxt = []
        hv = n // 2
        for half in range(2):
            vs = slice(half * hv, (half + 1) * hv)
            sa_h = sa[vs]
            vv = buf[_WV, off, vs, :]
            y = jnp.zeros((hv, LANES), F32)
            san = jnp.zeros((hv, LANES), F32)
            for k in range(n):
                row = slice(k, k + 1)
                sk = (st_ref[k, vs, :] * buf[_WW, off, row, :] + sa_h * buf[_WB, off, row, :]
                      + vv * buf[_WK, off, row, :])
                st_ref[k, vs, :] = sk
                y = y + sk * buf[_WR, off, row, :]
                san = san + sk * nbuf[_WA, noff, row, :]
            ybuf[off, vs, :] = y
            nxt.append(san)
        return jnp.concatenate(nxt, axis=0)

    rk = rk_ref[...]
    lng = lng_ref[...]
    lnb = lnb_ref[...]

    def finish(buf, off, ybuf):
        y = ybuf[off]
        mu = jnp.mean(y, axis=0, keepdims=True)
        yc = y - mu
        var = jnp.mean(yc * yc, axis=0, keepdims=True)
        yn = yc * lax.rsqrt(var + RW_LN_EPS) * lng + lnb
        bonus = jnp.sum(buf[_WR, off] * buf[_WK, off] * rk, axis=0, keepdims=True)
        return yn + bonus * buf[_WV, off]

    def store_pair(sp, buf, ybuf):
        rows = pl.ds(pl.multiple_of(sp * 16, 16), 16)
        y_ref[rows, :] = _from_lanes(finish(buf, 0, ybuf), finish(buf, 1, ybuf))

    load_pair(0, buf_a)
    sa0 = jnp.zeros((n, LANES), F32)
    for k in range(n):
        sa0 = sa0 + st_ref[k] * buf_a[_WA, 0, k:k + 1, :]

    def two_pairs(q, sa):
        sa = step(buf_a, 0, buf_a, 1, y_a, sa)
        load_pair(2 * q + 1, buf_b)
        sa = step(buf_a, 1, buf_b, 0, y_a, sa)
        sa = step(buf_b, 0, buf_b, 1, y_b, sa)
        store_pair(2 * q, buf_a, y_a)
        load_pair(jnp.minimum(2 * q + 2, pairs - 1), buf_a)
        sa = step(buf_b, 1, buf_a, 0, y_b, sa)
        store_pair(2 * q + 1, buf_b, y_b)
        return sa

    lax.fori_loop(0, pairs // 2, two_pairs, sa0)


def _wkv_call(r, w, k, v, kkraw, ar, rk_l, lng_l, lnb_l, batch, steps):
    t, width = r.shape
    rows = steps * batch
    n = RW_HEAD_DIM
    assert steps % 4 == 0
    pair_buf = pltpu.VMEM((6, 2, n, LANES), F32)
    pair_out = pltpu.VMEM((2, n, LANES), F32)
    return pl.pallas_call(
        functools.partial(_wkv_kernel, steps=steps),
        grid=(t // rows,),
        in_specs=[_row_spec(rows, width)] * 6 + [_const_spec((n, LANES))] * 3,
        out_specs=_row_spec(rows, width),
        out_shape=jax.ShapeDtypeStruct((t, width), F32),
        scratch_shapes=[pltpu.VMEM((n, n, LANES), F32), pair_buf, pair_buf, pair_out, pair_out],
        compiler_params=_cparams("arbitrary"),
        name="wkv7_scan",
    )(r, w, k, v, kkraw, ar, rk_l, lng_l, lnb_l)


def _head_param_lanes(pv, batch):
    a = pv.reshape(RW_HEADS // 2, 2, RW_HEAD_DIM).transpose(2, 1, 0)
    a = jnp.broadcast_to(a[..., None], a.shape + (batch,))
    return a.reshape(RW_HEAD_DIM, LANES).astype(F32)


def _gmlp_kernel(nx_ref, wu_ref, wg_ref, lng_ref, lnb_ref, ws_ref, bias_ref, yb_ref):
    nx = nx_ref[...]
    u = _gelu(_dot(nx, wu_ref[...]))
    gv = _gelu(_dot(nx, wg_ref[...]))
    mu = jnp.mean(gv, axis=-1, keepdims=True)
    gc = gv - mu
    var = jnp.mean(gc * gc, axis=-1, keepdims=True)
    v = (gc * lax.rsqrt(var + GM_LN_EPS) * lng_ref[...] + lnb_ref[...]).astype(BF16)
    gd = v.shape[1] // GM_GROUPS
    for g in range(GM_GROUPS):
        cols = slice(g * gd, (g + 1) * gd)
        s = _dot(ws_ref[g], v[:, cols]) + bias_ref[:, g:g + 1]
        yb_ref[:, cols] = (u[:, cols] * s).astype(BF16)


def _gmlp_call(nx, wu, wg, lng, lnb, ws_k, bias):
    t, d2 = nx.shape
    w = wu.shape[1]
    rows = ws_k.shape[1]
    return pl.pallas_call(
        _gmlp_kernel,
        grid=(t // rows,),
        in_specs=[_row_spec(rows, d2), _const_spec((d2, w)), _const_spec((d2, w)),
                  _const_spec((1, w)), _const_spec((1, w)),
                  _const_spec(ws_k.shape), _const_spec((rows, GM_GROUPS))],
        out_specs=_row_spec(rows, w),
        out_shape=jax.ShapeDtypeStruct((t, w), BF16),
        compiler_params=_cparams("parallel"),
        name="gmlp",
    )(nx, wu, wg, lng, lnb, ws_k, bias)


def _merge_kernel(nx_ref, y_ref, yb_ref, h_ref, wg_ref, g2_ref, wga_ref, wgb_ref,
                  wa_ref, wb_ref, wo_ref, nf_ref, hn_ref, n2_ref):
    nx = nx_ref[...]
    gate = _dot(_sigmoid(_dot(nx, wg_ref[...])).astype(BF16), g2_ref[...])
    ya = (y_ref[...] * gate).astype(BF16)
    pa = _dot(ya, wa_ref[...])
    pb = _dot(yb_ref[...], wb_ref[...])
    merged = _sigmoid(_dot(nx, wga_ref[...])) * pa + _sigmoid(_dot(nx, wgb_ref[...])) * pb
    hn = h_ref[...] + _dot(merged.astype(BF16), wo_ref[...])
    hn_ref[...] = hn
    n2_ref[...] = _rms(hn, nf_ref[...]).astype(BF16)


def _merge_call(nx, y, yb, h, wg, g2, wga, wgb, wa, wb, wo, nf, tm):
    t, d2 = nx.shape
    d = h.shape[1]
    w = y.shape[1]
    lg = wg.shape[1]
    return pl.pallas_call(
        _merge_kernel,
        grid=(t // tm,),
        in_specs=[_row_spec(tm, d2), _row_spec(tm, w), _row_spec(tm, w), _row_spec(tm, d),
                  _const_spec((d2, lg)), _const_spec((lg, w)),
                  _const_spec((d2, d)), _const_spec((d2, d)),
                  _const_spec((w, d)), _const_spec((w, d)), _const_spec((d, d)),
                  _const_spec((1, d))],
        out_specs=[_row_spec(tm, d), _row_spec(tm, d)],
        out_shape=[jax.ShapeDtypeStruct((t, d), F32), jax.ShapeDtypeStruct((t, d), BF16)],
        compiler_params=_cparams("parallel"),
        name="merge",
    )(nx, y, yb, h, wg, g2, wga, wgb, wa, wb, wo, nf)


def _top_values(s, count, rows):
    tops = []
    for _ in range(count):
        m = jnp.max(s, axis=0, keepdims=True)
        tops.append(m)
        s = jnp.where(s == m, -jnp.inf, s)
    tops += [jnp.full_like(tops[0], -jnp.inf)] * (rows - count)
    return jnp.concatenate(tops, axis=0)


def _peer_select_kernel(n2_ref, h_ref, p_ref, wq_ref, keys_ref, wple_ref, wpg_ref,
                        thr_ref, s2_ref, e1_ref, e2_ref, hp_ref, q_scr):
    n2 = n2_ref[...]
    ple = _dot(p_ref[...], wple_ref[...]) * _sigmoid(_dot(n2, wpg_ref[...]))
    hp_ref[...] = h_ref[...] + ple
    q = _dot(n2, wq_ref[...]).astype(BF16)
    for hp in range(2 * PEER_HEADS):
        q_scr[hp] = q[:, hp * PEER_HALF:(hp + 1) * PEER_HALF]
    tt = n2.shape[0]
    kk = PEER_TOPK
    nt = (((1,), (1,)), ((), ()))

    def head(h, _):
        s1 = lax.dot_general(keys_ref[h, 0], q_scr[2 * h], nt, preferred_element_type=F32)
        s2 = lax.dot_general(keys_ref[h, 1], q_scr[2 * h + 1], nt, preferred_element_type=F32)
        for c in range(tt // LANES):
            cols = slice(c * LANES, (c + 1) * LANES)
            s1c = s1[:, cols]
            s2c = s2[:, cols]
            t1 = _top_values(s1c, kk + 1, 24)
            t2 = _top_values(s2c, kk + 1, 24)
            cand = [t1[0:1] + t2]
            cand += [t1[i:i + 1] + t2[0:8] for i in range(1, 8)]
            cand += [t1[8:24] + t2[0:1]]
            best = _top_values(jnp.concatenate(cand, axis=0), kk + 1, kk + 1)
            top = best[0:kk]
            inv_z = 1.0 / jnp.sum(jnp.exp(top - top[0:1]), axis=0, keepdims=True)
            tau = 0.5 * (best[kk - 1:kk] + best[kk:kk + 1])
            per = PEER_CHUNK // LANES
            dst = (h, c // per, slice(None), slice((c % per) * LANES, (c % per + 1) * LANES))
            s2_ref[dst] = s2c
            thr_ref[dst] = tau - s1c
            e1_ref[dst] = jnp.exp(s1c - t1[0:1])
            e2_ref[dst] = jnp.exp(s2c - t2[0:1]) * inv_z
        return 0

    lax.fori_loop(0, PEER_HEADS, head, 0)


def _peer_select_call(n2, h, p, wq, keys, wple, wpg, tt):
    t, d = n2.shape
    pd = p.shape[1]
    dq = wq.shape[1]
    nh, nk = PEER_HEADS, PEER_KEYS
    tspec = pl.BlockSpec((nh, tt // PEER_CHUNK, nk, PEER_CHUNK), lambda i: (0, i, 0, 0))
    tshape = jax.ShapeDtypeStruct((nh, t // PEER_CHUNK, nk, PEER_CHUNK), F32)
    return pl.pallas_call(
        _peer_select_kernel,
        grid=(t // tt,),
        in_specs=[_row_spec(tt, d), _row_spec(tt, d), _row_spec(tt, pd),
                  _const_spec((d, dq)), _const_spec(keys.shape),
                  _const_spec((pd, d)), _const_spec((d, d))],
        out_specs=[tspec] * 4 + [_row_spec(tt, d)],
        out_shape=[tshape] * 4 + [jax.ShapeDtypeStruct((t, d), F32)],
        scratch_shapes=[pltpu.VMEM((2 * nh, tt, PEER_HALF), BF16)],
        compiler_params=_cparams("parallel"),
        name="peer_select",
    )(n2, h, p, wq, keys, wple, wpg)


def _peer_dense_kernel(n2_ref, u_ref, vt_ref, thr_ref, e1_ref, s2_ref, e2_ref, hp_ref,
                       out_ref, acc_ref, act0_ref, act1_ref, ga0_ref, ga1_ref, *, nj):
    f = pl.program_id(0)
    nt = (((1,), (1,)), ((), ()))
    nc, te, cw = act0_ref.shape
    nk = PEER_KEYS
    rb = 2 * SUBLANES
    j3 = jnp.maximum(f - 2, 0) % nj

    @pl.when(f == 0)
    def _():
        act1_ref[...] = jnp.zeros_like(act1_ref)
        ga0_ref[...] = jnp.zeros_like(ga0_ref)

    @pl.when(j3 == 0)
    def _():
        acc_ref[...] = jnp.zeros_like(acc_ref)

    def stages(act_new, act_cur, ga_cur, ga_old):
        for c in range(nc):
            tok = slice(c * cw, (c + 1) * cw)
            act_new[c] = lax.dot_general(u_ref[...], n2_ref[tok, :], nt,
                                         preferred_element_type=F32)
            for i1 in range(te // nk):
                thr = [thr_ref[h, c, i1:i1 + 1, :] for h in range(PEER_HEADS)]
                e1 = [e1_ref[h, c, i1:i1 + 1, :] for h in range(PEER_HEADS)]
                for r in range(nk // rb):
                    rs = slice(r * rb, (r + 1) * rb)
                    g = None
                    for h in range(PEER_HEADS):
                        keep = s2_ref[h, c, rs, :] >= thr[h]
                        term = jnp.where(keep, e2_ref[h, c, rs, :], 0.0) * e1[h]
                        g = term if g is None else g + term
                    rows = slice(i1 * nk + r * rb, i1 * nk + (r + 1) * rb)
                    ga_cur[c, rows, :] = (g * _gelu(act_cur[c, rows, :])).astype(BF16)
            acc_ref[c] += _dot(vt_ref[...], ga_old[c])

    @pl.when(f % 2 == 0)
    def _():
        stages(act0_ref, act1_ref, ga1_ref, ga0_ref)

    @pl.when(f % 2 == 1)
    def _():
        stages(act1_ref, act0_ref, ga0_ref, ga1_ref)

    @pl.when(jnp.logical_and(f >= 2, j3 == nj - 1))
    def _():
        for c in range(nc):
            out_ref[c * cw:(c + 1) * cw, :] = hp_ref[c * cw:(c + 1) * cw, :] + acc_ref[c].T


def _peer_dense_call(n2, u, vt, thr, e1, s2, e2, hp, tt, te):
    t, d = n2.shape
    ne = u.shape[0]
    nh, nk = PEER_HEADS, PEER_KEYS
    r1 = te // nk
    cw = PEER_CHUNK
    nc = tt // cw
    ni, nj = t // tt, ne // te
    last = ni * nj - 1

    def pair(f, lag):
        a = jnp.clip(f - lag, 0, last)
        return a // nj, a % nj

    score_blk = (nh, nc, nk, cw)
    row_blk = (nh, nc, r1, cw)
    return pl.pallas_call(
        functools.partial(_peer_dense_kernel, nj=nj),
        grid=(ni * nj + 2,),
        in_specs=[pl.BlockSpec((tt, d), lambda f: (pair(f, 0)[0], 0)),
                  pl.BlockSpec((te, d), lambda f: (pair(f, 0)[1], 0)),
                  pl.BlockSpec((d, te), lambda f: (0, pair(f, 2)[1])),
                  pl.BlockSpec(row_blk, lambda f: (0, pair(f, 1)[0], pair(f, 1)[1], 0)),
                  pl.BlockSpec(row_blk, lambda f: (0, pair(f, 1)[0], pair(f, 1)[1], 0)),
                  pl.BlockSpec(score_blk, lambda f: (0, pair(f, 1)[0], 0, 0)),
                  pl.BlockSpec(score_blk, lambda f: (0, pair(f, 1)[0], 0, 0)),
                  pl.BlockSpec((tt, d), lambda f: (pair(f, 2)[0], 0))],
        out_specs=pl.BlockSpec((tt, d), lambda f: (pair(f, 2)[0], 0)),
        out_shape=jax.ShapeDtypeStruct((t, d), F32),
        scratch_shapes=[pltpu.VMEM((nc, d, cw), F32)]
        + [pltpu.VMEM((nc, te, cw), F32)] * 2 + [pltpu.VMEM((nc, te, cw), BF16)] * 2,
        compiler_params=_cparams("arbitrary"),
        name="peer_dense",
    )(n2, u, vt, thr, e1, s2, e2, hp)


def _final_norm_kernel(h_ref, g_ref, o_ref):
    o_ref[0] = _rms(h_ref[...], g_ref[...])


def _final_norm_call(h2d, g, batch, seq, ts):
    d = g.shape[-1]
    return pl.pallas_call(
        _final_norm_kernel,
        grid=(seq // ts, batch),
        in_specs=[pl.BlockSpec((ts, d), lambda s, b: (s, b)), _const_spec((1, d))],
        out_specs=pl.BlockSpec((1, ts, d), lambda s, b: (b, s, 0)),
        out_shape=jax.ShapeDtypeStruct((batch, seq, d), F32),
        compiler_params=_cparams("parallel", "parallel"),
        name="final_norm",
    )(h2d, g.reshape(1, d))


def _tile(total, want):
    want = min(want, total)
    assert total % want == 0, (total, want)
    return want


def kernel(x, p, norm_mix, norm_ffn, norm_final, w_in, rw_w0, rw_w2, rw_a0, rw_a2, rw_g2, rw_kk, rw_ka, rw_rk, rw_ln_g, rw_ln_b, rw_mv_w1, rw_mv_w2, rw_mv_v0, gm_ln_g, gm_ln_b, gm_ws, gm_bs, w_proj_a, w_proj_b, w_out, peer_wq, peer_keys, peer_u, peer_v, ple_w, ple_gate):
    batch, seq, d = x.shape
    depth = w_in.shape[0]
    t = batch * seq
    w = RW_HEADS * RW_HEAD_DIM
    assert batch * RW_HEADS == LANES and batch == SUBLANES and d == w
    assert seq % GM_CHUNK == 0

    tm = _tile(t, 512)
    tt = _tile(t, 512)
    te = 1024
    steps = _tile(seq, 64)

    gw = gm_ln_g.shape[1]
    sizes = (w, w, w, rw_w2.shape[1], rw_a2.shape[1], rw_g2.shape[1], gw, gw, d, d)
    offs = [0]
    for sz in sizes:
        offs.append(offs[-1] + sz)

    def col(wi, idx):
        return wi[:, offs[idx]:offs[idx + 1]].astype(BF16)

    h = x.transpose(1, 0, 2).reshape(t, d)
    pt = p.transpose(0, 2, 1, 3).reshape(depth, t, p.shape[-1]).astype(BF16)
    row = lambda a: a.reshape(1, -1).astype(F32)
    eye = jnp.eye(batch, dtype=F32)
    tril = jnp.tril(jnp.ones((GM_CHUNK, GM_CHUNK), F32))

    v_first = None
    for i in range(depth):
        wi = w_in[i]
        nx = _norm_shift(h, norm_mix[i], batch, tm)

        r, k, kkraw, ar = _rk_call(nx, col(wi, 0), col(wi, 1), col(wi, 4), rw_a2[i].astype(BF16),
                                   row(rw_a0[i]), row(rw_kk[i]), row(rw_ka[i]), tm)
        mv = None
        if i > 0:
            mv = (rw_mv_w1[i - 1].astype(BF16), rw_mv_w2[i - 1].astype(BF16), row(rw_mv_v0[i - 1]))
        v, dec = _vw_call(nx, col(wi, 2), col(wi, 3), rw_w2[i].astype(BF16), row(rw_w0[i]), mv,
                          v_first, tm)
        if i == 0:
            v_first = v

        y = _wkv_call(r, dec, k, v, kkraw, ar,
                      _head_param_lanes(rw_rk[i], batch),
                      _head_param_lanes(rw_ln_g[i].reshape(RW_HEADS, RW_HEAD_DIM), batch),
                      _head_param_lanes(rw_ln_b[i].reshape(RW_HEADS, RW_HEAD_DIM), batch),
                      batch, steps)

        ws_k = jax.vmap(lambda m: jnp.kron(m * tril, eye))(gm_ws[i]).astype(BF16)
        bias = jnp.repeat(gm_bs[i].T, batch, axis=0).astype(F32)
        yb = _gmlp_call(nx, col(wi, 6), col(wi, 7), row(gm_ln_g[i]), row(gm_ln_b[i]), ws_k, bias)

        h, n2 = _merge_call(nx, y, yb, h, col(wi, 5), rw_g2[i].astype(BF16), col(wi, 8), col(wi, 9),
                            w_proj_a[i].astype(BF16), w_proj_b[i].astype(BF16),
                            w_out[i].astype(BF16), row(norm_ffn[i]), tm)

        thr, s2, e1, e2, hp = _peer_select_call(
            n2, h, pt[i], peer_wq[i].astype(BF16), peer_keys[i].astype(BF16),
            ple_w[i].astype(BF16), ple_gate[i].astype(BF16), tt)
        h = _peer_dense_call(n2, peer_u[i].astype(BF16), peer_v[i].T.astype(BF16),
                             thr, e1, s2, e2, hp, tt, te)

    return _final_norm_call(h.reshape(seq, batch * d), norm_final, batch, seq, _tile(seq, 512))
```

```python
import functools
import math

import jax
import jax.numpy as jnp
from jax import lax
from jax.experimental import pallas as pl
from jax.experimental.pallas import tpu as pltpu

F32 = jnp.float32
BF16 = jnp.bfloat16

RW_HEADS = 16
RW_HEAD_DIM = 64
RW_LN_EPS = 64e-5
GM_GROUPS = 8
GM_CHUNK = 128
GM_LN_EPS = 1e-5
PEER_HEADS = 8
PEER_KEYS = 128
PEER_HALF = 128
PEER_TOPK = 16
PEER_CHUNK = 256
RMS_EPS = 1e-6

LANES = 128
SUBLANES = 8
VMEM_LIMIT = 56 * 1024 * 1024


def _cparams(*sem):
    return pltpu.CompilerParams(dimension_semantics=tuple(sem), vmem_limit_bytes=VMEM_LIMIT)


def _const_spec(shape):
    nd = len(shape)
    return pl.BlockSpec(shape, lambda *_: (0,) * nd, pipeline_mode=pl.Buffered(1))


def _row_spec(rows, cols):
    return pl.BlockSpec((rows, cols), lambda i: (i, 0))


def _gelu(x):
    c = math.sqrt(2.0 / math.pi)
    return 0.5 * x * (1.0 + jnp.tanh(c * (x + 0.044715 * (x * x * x))))


def _sigmoid(x):
    return 1.0 / (1.0 + jnp.exp(-x))


def _dot(a, b):
    return jnp.dot(a, b, preferred_element_type=F32)


def _rms(x, g):
    return x * lax.rsqrt(jnp.mean(x * x, axis=-1, keepdims=True) + RMS_EPS) * g


def _norm_shift_kernel(h_ref, hprev_ref, g_ref, nx_ref, *, batch):
    i = pl.program_id(0)
    g = g_ref[...]
    n = _rms(h_ref[...], g)
    prev = _rms(hprev_ref[...], g) * (i > 0).astype(F32)
    d = n.shape[1]
    shifted = jnp.concatenate([prev, n[:-batch]], axis=0)
    nx_ref[:, :d] = n.astype(BF16)
    nx_ref[:, d:] = shifted.astype(BF16)


def _norm_shift(h, g, batch, tm):
    t, d = h.shape
    per = tm // batch
    return pl.pallas_call(
        functools.partial(_norm_shift_kernel, batch=batch),
        grid=(t // tm,),
        in_specs=[_row_spec(tm, d),
                  pl.BlockSpec((batch, d), lambda i: (jnp.maximum(i * per - 1, 0), 0)),
                  _const_spec((1, d))],
        out_specs=_row_spec(tm, 2 * d),
        out_shape=jax.ShapeDtypeStruct((t, 2 * d), BF16),
        compiler_params=_cparams("parallel"),
        name="norm_shift",
    )(h, h, g.reshape(1, d))


def _rk_kernel(nx_ref, wr_ref, wk_ref, wa_ref, a2_ref, a0_ref, kk_ref, ka_ref,
               r_ref, k_ref, kkraw_ref, ar_ref):
    nx = nx_ref[...]
    r_ref[...] = _dot(nx, wr_ref[...])
    zk = _dot(nx, wk_ref[...])
    za = _dot(nx, wa_ref[...])
    a_rate = _sigmoid(a0_ref[...] + _dot(za.astype(BF16), a2_ref[...]))
    ar_ref[...] = a_rate
    kkraw_ref[...] = zk * kk_ref[...]
    k_ref[...] = zk * (1.0 + (a_rate - 1.0) * ka_ref[...])


def _rk_call(nx, wr, wk, wa, a2, a0, kk, ka, tm):
    t, d2 = nx.shape
    w = wr.shape[1]
    la = wa.shape[1]
    out = jax.ShapeDtypeStruct((t, w), F32)
    return pl.pallas_call(
        _rk_kernel,
        grid=(t // tm,),
        in_specs=[_row_spec(tm, d2), _const_spec((d2, w)), _const_spec((d2, w)),
                  _const_spec((d2, la)), _const_spec((la, w)),
                  _const_spec((1, w)), _const_spec((1, w)), _const_spec((1, w))],
        out_specs=[_row_spec(tm, w)] * 4,
        out_shape=[out] * 4,
        compiler_params=_cparams("parallel"),
        name="rwkv_rk",
    )(nx, wr, wk, wa, a2, a0, kk, ka)


def _decay_from(nx, ww_ref, w2_ref, w0_ref):
    zw = _dot(nx, ww_ref[...])
    x = -(w0_ref[...] + _dot(jnp.tanh(zw).astype(BF16), w2_ref[...]))
    softplus = jnp.maximum(x, 0.0) + jnp.log(1.0 + jnp.exp(-jnp.abs(x)))
    return jnp.exp(-jnp.exp(-softplus - 0.5))


def _vw0_kernel(nx_ref, wv_ref, ww_ref, w2_ref, w0_ref, v_ref, dec_ref):
    nx = nx_ref[...]
    v_ref[...] = _dot(nx, wv_ref[...])
    dec_ref[...] = _decay_from(nx, ww_ref, w2_ref, w0_ref)


def _vw_kernel(nx_ref, wv_ref, ww_ref, w2_ref, w0_ref, m1_ref, m2_ref, v0_ref, vfirst_ref,
               v_ref, dec_ref):
    nx = nx_ref[...]
    zv = _dot(nx, wv_ref[...])
    lo = _dot(nx, m1_ref[...])
    vmix = _sigmoid(v0_ref[...] + _dot(lo.astype(BF16), m2_ref[...]))
    v_ref[...] = zv + (vfirst_ref[...] - zv) * vmix
    dec_ref[...] = _decay_from(nx, ww_ref, w2_ref, w0_ref)


def _vw_call(nx, wv, ww, w2, w0, mv, vfirst, tm):
    t, d2 = nx.shape
    w = wv.shape[1]
    lw = ww.shape[1]
    out = jax.ShapeDtypeStruct((t, w), F32)
    in_specs = [_row_spec(tm, d2), _const_spec((d2, w)), _const_spec((d2, lw)),
                _const_spec((lw, w)), _const_spec((1, w))]
    args = [nx, wv, ww, w2, w0]
    if mv is None:
        body = _vw0_kernel
    else:
        m1, m2, v0 = mv
        lm = m1.shape[1]
        body = _vw_kernel
        in_specs += [_const_spec((d2, lm)), _const_spec((lm, w)), _const_spec((1, w)),
                     _row_spec(tm, w)]
        args += [m1, m2, v0, vfirst]
    return pl.pallas_call(
        body,
        grid=(t // tm,),
        in_specs=in_specs,
        out_specs=[_row_spec(tm, w)] * 2,
        out_shape=[out] * 2,
        compiler_params=_cparams("parallel"),
        name="rwkv_vw",
    )(*args)


def _to_lanes(blk):
    r = jnp.concatenate([blk[s * 8:(s + 1) * 8, j * LANES:(j + 1) * LANES]
                         for s in range(2) for j in range(8)], axis=0)
    tr = r.T
    lo, hi = tr[0:64], tr[64:128]
    lane = lax.broadcasted_iota(jnp.int32, (64, LANES), 1)
    first = lane < 64
    return (jnp.where(first, lo, pltpu.roll(hi, 64, 1)),
            jnp.where(first, pltpu.roll(lo, 64, 1), hi))


def _from_lanes(o0, o1):
    lane = lax.broadcasted_iota(jnp.int32, (64, LANES), 1)
    first = lane < 64
    lo = jnp.where(first, o0, pltpu.roll(o1, 64, 1))
    hi = jnp.where(first, pltpu.roll(o0, 64, 1), o1)
    r = jnp.concatenate([lo, hi], axis=0).T
    rows = [jnp.concatenate([r[s * 64 + j * 8:s * 64 + j * 8 + 8] for j in range(8)], axis=1)
            for s in range(2)]
    return jnp.concatenate(rows, axis=0)


_WR, _WW, _WK, _WV, _WA, _WB = range(6)


def _wkv_kernel(r_ref, w_ref, k_ref, v_ref, kk_ref, ar_ref, rk_ref, lng_ref, lnb_ref,
                y_ref, st_ref, buf_a, buf_b, y_a, y_b, *, steps):
    n = RW_HEAD_DIM
    pairs = steps // 2

    @pl.when(pl.program_id(0) == 0)
    def _():
        st_ref[...] = jnp.zeros_like(st_ref)

    def load_pair(sp, buf):
        rows = pl.ds(pl.multiple_of(sp * 16, 16), 16)
        for idx, src in ((_WR, r_ref), (_WW, w_ref), (_WK, k_ref), (_WV, v_ref)):
            o0, o1 = _to_lanes(src[rows, :])
            buf[idx, 0] = o0
            buf[idx, 1] = o1
        kk0, kk1 = _to_lanes(kk_ref[rows, :])
        ar0, ar1 = _to_lanes(ar_ref[rows, :])
        for off, kk, ar in ((0, kk0, ar0), (1, kk1, ar1)):
            ss = jnp.sum(kk * kk, axis=0, keepdims=True)
            kkn = kk * lax.rsqrt(jnp.maximum(ss, 1e-24))
            buf[_WA, off] = -kkn
            buf[_WB, off] = kkn * ar

    def step(buf, off, nbuf, noff, ybuf, sa):
        nxt = []
        hv = n // 2
        for half in range(2):
            vs = slice(half * hv, (half + 1) * hv)
            sa_h = sa[vs]
            vv = buf[_WV, off, vs, :]
            y = jnp.zeros((hv, LANES), F32)
            san = jnp.zeros((hv, LANES), F32)
            for k in range(n):
                row = slice(k, k + 1)
                sk = (st_ref[k, vs, :] * buf[_WW, off, row, :] + sa_h * buf[_WB, off, row, :]
                      + vv * buf[_WK, off, row, :])
                st_ref[k, vs, :] = sk
                y = y + sk * buf[_WR, off, row, :]
                san = san + sk * nbuf[_WA, noff, row, :]
            ybuf[off, vs, :] = y
            nxt.append(san)
        return jnp.concatenate(nxt, axis=0)

    rk = rk_ref[...]
    lng = lng_ref[...]
    lnb = lnb_ref[...]

    def finish(buf, off, ybuf):
        y = ybuf[off]
        mu = jnp.mean(y, axis=0, keepdims=True)
        yc = y - mu
        var = jnp.mean(yc * yc, axis=0, keepdims=True)
        yn = yc * lax.rsqrt(var + RW_LN_EPS) * lng + lnb
        bonus = jnp.sum(buf[_WR, off] * buf[_WK, off] * rk, axis=0, keepdims=True)
        return yn + bonus * buf[_WV, off]

    def store_pair(sp, buf, ybuf):
        rows = pl.ds(pl.multiple_of(sp * 16, 16), 16)
        y_ref[rows, :] = _from_lanes(finish(buf, 0, ybuf), finish(buf, 1, ybuf))

    load_pair(0, buf_a)
    sa0 = jnp.zeros((n, LANES), F32)
    for k in range(n):
        sa0 = sa0 + st_ref[k] * buf_a[_WA, 0, k:k + 1, :]

    def two_pairs(q, sa):
        sa = step(buf_a, 0, buf_a, 1, y_a, sa)
        load_pair(2 * q + 1, buf_b)
        sa = step(buf_a, 1, buf_b, 0, y_a, sa)
        sa = step(buf_b, 0, buf_b, 1, y_b, sa)
        store_pair(2 * q, buf_a, y_a)
        load_pair(jnp.minimum(2 * q + 2, pairs - 1), buf_a)
        sa = step(buf_b, 1, buf_a, 0, y_b, sa)
        store_pair(2 * q + 1, buf_b, y_b)
        return sa

    lax.fori_loop(0, pairs // 2, two_pairs, sa0)


def _wkv_call(r, w, k, v, kkraw, ar, rk_l, lng_l, lnb_l, batch, steps):
    t, width = r.shape
    rows = steps * batch
    n = RW_HEAD_DIM
    assert steps % 4 == 0
    pair_buf = pltpu.VMEM((6, 2, n, LANES), F32)
    pair_out = pltpu.VMEM((2, n, LANES), F32)
    return pl.pallas_call(
        functools.partial(_wkv_kernel, steps=steps),
        grid=(t // rows,),
        in_specs=[_row_spec(rows, width)] * 6 + [_const_spec((n, LANES))] * 3,
        out_specs=_row_spec(rows, width),
        out_shape=jax.ShapeDtypeStruct((t, width), F32),
        scratch_shapes=[pltpu.VMEM((n, n, LANES), F32), pair_buf, pair_buf, pair_out, pair_out],
        compiler_params=_cparams("arbitrary"),
        name="wkv7_scan",
    )(r, w, k, v, kkraw, ar, rk_l, lng_l, lnb_l)


def _head_param_lanes(pv, batch):
    a = pv.reshape(RW_HEADS // 2, 2, RW_HEAD_DIM).transpose(2, 1, 0)
    a = jnp.broadcast_to(a[..., None], a.shape + (batch,))
    return a.reshape(RW_HEAD_DIM, LANES).astype(F32)


def _gmlp_kernel(nx_ref, wu_ref, wg_ref, lng_ref, lnb_ref, ws_ref, bias_ref, yb_ref):
    nx = nx_ref[...]
    u = _gelu(_dot(nx, wu_ref[...]))
    gv = _gelu(_dot(nx, wg_ref[...]))
    mu = jnp.mean(gv, axis=-1, keepdims=True)
    gc = gv - mu
    var = jnp.mean(gc * gc, axis=-1, keepdims=True)
    v = (gc * lax.rsqrt(var + GM_LN_EPS) * lng_ref[...] + lnb_ref[...]).astype(BF16)
    gd = v.shape[1] // GM_GROUPS
    for g in range(GM_GROUPS):
        cols = slice(g * gd, (g + 1) * gd)
        s = _dot(ws_ref[g], v[:, cols]) + bias_ref[:, g:g + 1]
        yb_ref[:, cols] = (u[:, cols] * s).astype(BF16)


def _gmlp_call(nx, wu, wg, lng, lnb, ws_k, bias):
    t, d2 = nx.shape
    w = wu.shape[1]
    rows = ws_k.shape[1]
    return pl.pallas_call(
        _gmlp_kernel,
        grid=(t // rows,),
        in_specs=[_row_spec(rows, d2), _const_spec((d2, w)), _const_spec((d2, w)),
                  _const_spec((1, w)), _const_spec((1, w)),
                  _const_spec(ws_k.shape), _const_spec((rows, GM_GROUPS))],
        out_specs=_row_spec(rows, w),
        out_shape=jax.ShapeDtypeStruct((t, w), BF16),
        compiler_params=_cparams("parallel"),
        name="gmlp",
    )(nx, wu, wg, lng, lnb, ws_k, bias)


def _merge_kernel(nx_ref, y_ref, yb_ref, h_ref, wg_ref, g2_ref, wga_ref, wgb_ref,
                  wa_ref, wb_ref, wo_ref, nf_ref, hn_ref, n2_ref):
    nx = nx_ref[...]
    gate = _dot(_sigmoid(_dot(nx, wg_ref[...])).astype(BF16), g2_ref[...])
    ya = (y_ref[...] * gate).astype(BF16)
    pa = _dot(ya, wa_ref[...])
    pb = _dot(yb_ref[...], wb_ref[...])
    merged = _sigmoid(_dot(nx, wga_ref[...])) * pa + _sigmoid(_dot(nx, wgb_ref[...])) * pb
    hn = h_ref[...] + _dot(merged.astype(BF16), wo_ref[...])
    hn_ref[...] = hn
    n2_ref[...] = _rms(hn, nf_ref[...]).astype(BF16)


def _merge_call(nx, y, yb, h, wg, g2, wga, wgb, wa, wb, wo, nf, tm):
    t, d2 = nx.shape
    d = h.shape[1]
    w = y.shape[1]
    lg = wg.shape[1]
    return pl.pallas_call(
        _merge_kernel,
        grid=(t // tm,),
        in_specs=[_row_spec(tm, d2), _row_spec(tm, w), _row_spec(tm, w), _row_spec(tm, d),
                  _const_spec((d2, lg)), _const_spec((lg, w)),
                  _const_spec((d2, d)), _const_spec((d2, d)),
                  _const_spec((w, d)), _const_spec((w, d)), _const_spec((d, d)),
                  _const_spec((1, d))],
        out_specs=[_row_spec(tm, d), _row_spec(tm, d)],
        out_shape=[jax.ShapeDtypeStruct((t, d), F32), jax.ShapeDtypeStruct((t, d), BF16)],
        compiler_params=_cparams("parallel"),
        name="merge",
    )(nx, y, yb, h, wg, g2, wga, wgb, wa, wb, wo, nf)


def _top_values(s, count, rows):
    tops = []
    for _ in range(count):
        m = jnp.max(s, axis=0, keepdims=True)
        tops.append(m)
        s = jnp.where(s == m, -jnp.inf, s)
    tops += [jnp.full_like(tops[0], -jnp.inf)] * (rows - count)
    return jnp.concatenate(tops, axis=0)


def _peer_select_kernel(n2_ref, h_ref, p_ref, wq_ref, keys_ref, wple_ref, wpg_ref,
                        thr_ref, s2_ref, e1_ref, e2_ref, hp_ref, q_scr):
    n2 = n2_ref[...]
    ple = _dot(p_ref[...], wple_ref[...]) * _sigmoid(_dot(n2, wpg_ref[...]))
    hp_ref[...] = h_ref[...] + ple
    q = _dot(n2, wq_ref[...]).astype(BF16)
    for hp in range(2 * PEER_HEADS):
        q_scr[hp] = q[:, hp * PEER_HALF:(hp + 1) * PEER_HALF]
    tt = n2.shape[0]
    kk = PEER_TOPK
    nt = (((1,), (1,)), ((), ()))

    def head(h, _):
        s1 = lax.dot_general(keys_ref[h, 0], q_scr[2 * h], nt, preferred_element_type=F32)
        s2 = lax.dot_general(keys_ref[h, 1], q_scr[2 * h + 1], nt, preferred_element_type=F32)
        for c in range(tt // LANES):
            cols = slice(c * LANES, (c + 1) * LANES)
            s1c = s1[:, cols]
            s2c = s2[:, cols]
            t1 = _top_values(s1c, kk + 1, 24)
            t2 = _top_values(s2c, kk + 1, 24)
            cand = [t1[0:1] + t2]
            cand += [t1[i:i + 1] + t2[0:8] for i in range(1, 8)]
            cand += [t1[8:24] + t2[0:1]]
            best = _top_values(jnp.concatenate(cand, axis=0), kk + 1, kk + 1)
            top = best[0:kk]
            inv_z = 1.0 / jnp.sum(jnp.exp(top - top[0:1]), axis=0, keepdims=True)
            tau = 0.5 * (best[kk - 1:kk] + best[kk:kk + 1])
            per = PEER_CHUNK // LANES
            dst = (h, c // per, slice(None), slice((c % per) * LANES, (c % per + 1) * LANES))
            s2_ref[dst] = s2c
            thr_ref[dst] = tau - s1c
            e1_ref[dst] = jnp.exp(s1c - t1[0:1])
            e2_ref[dst] = jnp.exp(s2c - t2[0:1]) * inv_z
        return 0

    lax.fori_loop(0, PEER_HEADS, head, 0)


def _peer_select_call(n2, h, p, wq, keys, wple, wpg, tt):
    t, d = n2.shape
    pd = p.shape[1]
    dq = wq.shape[1]
    nh, nk = PEER_HEADS, PEER_KEYS
    tspec = pl.BlockSpec((nh, tt // PEER_CHUNK, nk, PEER_CHUNK), lambda i: (0, i, 0, 0))
    tshape = jax.ShapeDtypeStruct((nh, t // PEER_CHUNK, nk, PEER_CHUNK), F32)
    return pl.pallas_call(
        _peer_select_kernel,
        grid=(t // tt,),
        in_specs=[_row_spec(tt, d), _row_spec(tt, d), _row_spec(tt, pd),
                  _const_spec((d, dq)), _const_spec(keys.shape),
                  _const_spec((pd, d)), _const_spec((d, d))],
        out_specs=[tspec] * 4 + [_row_spec(tt, d)],
        out_shape=[tshape] * 4 + [jax.ShapeDtypeStruct((t, d), F32)],
        scratch_shapes=[pltpu.VMEM((2 * nh, tt, PEER_HALF), BF16)],
        compiler_params=_cparams("parallel"),
        name="peer_select",
    )(n2, h, p, wq, keys, wple, wpg)


def _peer_dense_kernel(n2_ref, u_ref, vt_ref, thr_ref, e1_ref, s2_ref, e2_ref, hp_ref,
                       out_ref, acc_ref, act0_ref, act1_ref, ga0_ref, ga1_ref, *, nj):
    f = pl.program_id(0)
    nt = (((1,), (1,)), ((), ()))
    nc, te, cw = act0_ref.shape
    nk = PEER_KEYS
    rb = 2 * SUBLANES
    j3 = jnp.maximum(f - 2, 0) % nj

    @pl.when(f == 0)
    def _():
        act1_ref[...] = jnp.zeros_like(act1_ref)
        ga0_ref[...] = jnp.zeros_like(ga0_ref)

    @pl.when(j3 == 0)
    def _():
        acc_ref[...] = jnp.zeros_like(acc_ref)

    def stages(act_new, act_cur, ga_cur, ga_old):
        def chunk(c, _):
            tok = pl.ds(pl.multiple_of(c * cw, cw), cw)
            act_new[c] = lax.dot_general(u_ref[...], n2_ref[tok, :], nt,
                                         preferred_element_type=F32)
            for i1 in range(te // nk):
                thr = [thr_ref[h, c, i1:i1 + 1, :] for h in range(PEER_HEADS)]
                e1 = [e1_ref[h, c, i1:i1 + 1, :] for h in range(PEER_HEADS)]
                for r in range(nk // rb):
                    rs = slice(r * rb, (r + 1) * rb)
                    g = None
                    for h in range(PEER_HEADS):
                        keep = s2_ref[h, c, rs, :] >= thr[h]
                        term = jnp.where(keep, e2_ref[h, c, rs, :], 0.0) * e1[h]
                        g = term if g is None else g + term
                    rows = slice(i1 * nk + r * rb, i1 * nk + (r + 1) * rb)
                    ga_cur[c, rows, :] = (g * _gelu(act_cur[c, rows, :])).astype(BF16)
            acc_ref[c] += lax.dot_general(vt_ref[...], ga_old[c], (((0,), (0,)), ((), ())),
                                          preferred_element_type=F32)
            return 0

        lax.fori_loop(0, nc, chunk, 0)

    @pl.when(f % 2 == 0)
    def _():
        stages(act0_ref, act1_ref, ga1_ref, ga0_ref)

    @pl.when(f % 2 == 1)
    def _():
        stages(act1_ref, act0_ref, ga0_ref, ga1_ref)

    @pl.when(jnp.logical_and(f >= 2, j3 == nj - 1))
    def _():
        for c in range(nc):
            out_ref[c * cw:(c + 1) * cw, :] = hp_ref[c * cw:(c + 1) * cw, :] + acc_ref[c].T


def _peer_dense_call(n2, u, vt, thr, e1, s2, e2, hp, tt, te):
    t, d = n2.shape
    ne = u.shape[0]
    nh, nk = PEER_HEADS, PEER_KEYS
    r1 = te // nk
    cw = PEER_CHUNK
    nc = tt // cw
    ni, nj = t // tt, ne // te
    last = ni * nj - 1

    def pair(f, lag):
        a = jnp.clip(f - lag, 0, last)
        return a // nj, a % nj

    score_blk = (nh, nc, nk, cw)
    row_blk = (nh, nc, r1, cw)
    return pl.pallas_call(
        functools.partial(_peer_dense_kernel, nj=nj),
        grid=(ni * nj + 2,),
        in_specs=[pl.BlockSpec((tt, d), lambda f: (pair(f, 0)[0], 0)),
                  pl.BlockSpec((te, d), lambda f: (pair(f, 0)[1], 0)),
                  pl.BlockSpec((te, d), lambda f: (pair(f, 2)[1], 0)),
                  pl.BlockSpec(row_blk, lambda f: (0, pair(f, 1)[0], pair(f, 1)[1], 0)),
                  pl.BlockSpec(row_blk, lambda f: (0, pair(f, 1)[0], pair(f, 1)[1], 0)),
                  pl.BlockSpec(score_blk, lambda f: (0, pair(f, 1)[0], 0, 0)),
                  pl.BlockSpec(score_blk, lambda f: (0, pair(f, 1)[0], 0, 0)),
                  pl.BlockSpec((tt, d), lambda f: (pair(f, 2)[0], 0))],
        out_specs=pl.BlockSpec((tt, d), lambda f: (pair(f, 2)[0], 0)),
        out_shape=jax.ShapeDtypeStruct((t, d), F32),
        scratch_shapes=[pltpu.VMEM((nc, d, cw), F32)]
        + [pltpu.VMEM((nc, te, cw), F32)] * 2 + [pltpu.VMEM((nc, te, cw), BF16)] * 2,
        compiler_params=_cparams("arbitrary"),
        name="peer_dense",
    )(n2, u, vt, thr, e1, s2, e2, hp)


def _final_norm_kernel(h_ref, g_ref, o_ref):
    o_ref[0] = _rms(h_ref[...], g_ref[...])


def _final_norm_call(h2d, g, batch, seq, ts):
    d = g.shape[-1]
    return pl.pallas_call(
        _final_norm_kernel,
        grid=(seq // ts, batch),
        in_specs=[pl.BlockSpec((ts, d), lambda s, b: (s, b)), _const_spec((1, d))],
        out_specs=pl.BlockSpec((1, ts, d), lambda s, b: (b, s, 0)),
        out_shape=jax.ShapeDtypeStruct((batch, seq, d), F32),
        compiler_params=_cparams("parallel", "parallel"),
        name="final_norm",
    )(h2d, g.reshape(1, d))


def _tile(total, want):
    want = min(want, total)
    assert total % want == 0, (total, want)
    return want


def kernel(x, p, norm_mix, norm_ffn, norm_final, w_in, rw_w0, rw_w2, rw_a0, rw_a2, rw_g2, rw_kk, rw_ka, rw_rk, rw_ln_g, rw_ln_b, rw_mv_w1, rw_mv_w2, rw_mv_v0, gm_ln_g, gm_ln_b, gm_ws, gm_bs, w_proj_a, w_proj_b, w_out, peer_wq, peer_keys, peer_u, peer_v, ple_w, ple_gate):
    batch, seq, d = x.shape
    depth = w_in.shape[0]
    t = batch * seq
    w = RW_HEADS * RW_HEAD_DIM
    assert batch * RW_HEADS == LANES and batch == SUBLANES and d == w
    assert seq % GM_CHUNK == 0

    tm = _tile(t, 512)
    tt = _tile(t, 512)
    te = 1024
    steps = _tile(seq, 64)

    gw = gm_ln_g.shape[1]
    sizes = (w, w, w, rw_w2.shape[1], rw_a2.shape[1], rw_g2.shape[1], gw, gw, d, d)
    offs = [0]
    for sz in sizes:
        offs.append(offs[-1] + sz)

    def col(wi, idx):
        return wi[:, offs[idx]:offs[idx + 1]].astype(BF16)

    h = x.transpose(1, 0, 2).reshape(t, d)
    pt = p.transpose(0, 2, 1, 3).reshape(depth, t, p.shape[-1]).astype(BF16)
    row = lambda a: a.reshape(1, -1).astype(F32)
    eye = jnp.eye(batch, dtype=BF16)
    tril = jnp.tril(jnp.ones((GM_CHUNK, GM_CHUNK), F32))

    v_first = None
    for i in range(depth):
        wi = w_in[i]
        nx = _norm_shift(h, norm_mix[i], batch, tm)

        r, k, kkraw, ar = _rk_call(nx, col(wi, 0), col(wi, 1), col(wi, 4), rw_a2[i].astype(BF16),
                                   row(rw_a0[i]), row(rw_kk[i]), row(rw_ka[i]), tm)
        mv = None
        if i > 0:
            mv = (rw_mv_w1[i - 1].astype(BF16), rw_mv_w2[i - 1].astype(BF16), row(rw_mv_v0[i - 1]))
        v, dec = _vw_call(nx, col(wi, 2), col(wi, 3), rw_w2[i].astype(BF16), row(rw_w0[i]), mv,
                          v_first, tm)
        if i == 0:
            v_first = v

        y = _wkv_call(r, dec, k, v, kkraw, ar,
                      _head_param_lanes(rw_rk[i], batch),
                      _head_param_lanes(rw_ln_g[i].reshape(RW_HEADS, RW_HEAD_DIM), batch),
                      _head_param_lanes(rw_ln_b[i].reshape(RW_HEADS, RW_HEAD_DIM), batch),
                      batch, steps)

        ws_b = (gm_ws[i] * tril).astype(BF16)
        ws_k = (ws_b[:, :, None, :, None] * eye[None, None, :, None, :]).reshape(
            GM_GROUPS, GM_CHUNK * batch, GM_CHUNK * batch)
        bias = jnp.repeat(gm_bs[i].T, batch, axis=0).astype(F32)
        yb = _gmlp_call(nx, col(wi, 6), col(wi, 7), row(gm_ln_g[i]), row(gm_ln_b[i]), ws_k, bias)

        h, n2 = _merge_call(nx, y, yb, h, col(wi, 5), rw_g2[i].astype(BF16), col(wi, 8), col(wi, 9),
                            w_proj_a[i].astype(BF16), w_proj_b[i].astype(BF16),
                            w_out[i].astype(BF16), row(norm_ffn[i]), tm)

        thr, s2, e1, e2, hp = _peer_select_call(
            n2, h, pt[i], peer_wq[i].astype(BF16), peer_keys[i].astype(BF16),
            ple_w[i].astype(BF16), ple_gate[i].astype(BF16), tt)
        h = _peer_dense_call(n2, peer_u[i].astype(BF16), peer_v[i].astype(BF16),
                             thr, e1, s2, e2, hp, tt, te)

    return _final_norm_call(h.reshape(seq, batch * d), norm_final, batch, seq, _tile(seq, 512))
```

```python
import functools
import math

import jax
import jax.numpy as jnp
from jax import lax
from jax.experimental import pallas as pl
from jax.experimental.pallas import tpu as pltpu

F32 = jnp.float32
BF16 = jnp.bfloat16

RW_HEADS = 16
RW_HEAD_DIM = 64
RW_LN_EPS = 64e-5
GM_GROUPS = 8
GM_CHUNK = 128
GM_LN_EPS = 1e-5
PEER_HEADS = 8
PEER_KEYS = 128
PEER_HALF = 128
PEER_TOPK = 16
PEER_CHUNK = 256
RMS_EPS = 1e-6

LANES = 128
SUBLANES = 8
VMEM_LIMIT = 56 * 1024 * 1024


def _cparams(*sem):
    return pltpu.CompilerParams(dimension_semantics=tuple(sem), vmem_limit_bytes=VMEM_LIMIT)


def _const_spec(shape):
    nd = len(shape)
    return pl.BlockSpec(shape, lambda *_: (0,) * nd, pipeline_mode=pl.Buffered(1))


def _row_spec(rows, cols):
    return pl.BlockSpec((rows, cols), lambda i: (i, 0))


def _gelu(x):
    c = math.sqrt(2.0 / math.pi)
    return 0.5 * x * (1.0 + jnp.tanh(c * (x + 0.044715 * (x * x * x))))


def _sigmoid(x):
    return 1.0 / (1.0 + jnp.exp(-x))


def _dot(a, b):
    return jnp.dot(a, b, preferred_element_type=F32)


def _rms(x, g):
    return x * lax.rsqrt(jnp.mean(x * x, axis=-1, keepdims=True) + RMS_EPS) * g


def _norm_shift_kernel(h_ref, hprev_ref, g_ref, nx_ref, *, batch):
    i = pl.program_id(0)
    g = g_ref[...]
    n = _rms(h_ref[...], g)
    prev = _rms(hprev_ref[...], g) * (i > 0).astype(F32)
    d = n.shape[1]
    shifted = jnp.concatenate([prev, n[:-batch]], axis=0)
    nx_ref[:, :d] = n.astype(BF16)
    nx_ref[:, d:] = shifted.astype(BF16)


def _norm_shift(h, g, batch, tm):
    t, d = h.shape
    per = tm // batch
    return pl.pallas_call(
        functools.partial(_norm_shift_kernel, batch=batch),
        grid=(t // tm,),
        in_specs=[_row_spec(tm, d),
                  pl.BlockSpec((batch, d), lambda i: (jnp.maximum(i * per - 1, 0), 0)),
                  _const_spec((1, d))],
        out_specs=_row_spec(tm, 2 * d),
        out_shape=jax.ShapeDtypeStruct((t, 2 * d), BF16),
        compiler_params=_cparams("parallel"),
        name="norm_shift",
    )(h, h, g.reshape(1, d))


def _rk_kernel(nx_ref, wr_ref, wk_ref, wa_ref, a2_ref, a0_ref, kk_ref, ka_ref,
               r_ref, k_ref, kkraw_ref, ar_ref):
    nx = nx_ref[...]
    r_ref[...] = _dot(nx, wr_ref[...])
    zk = _dot(nx, wk_ref[...])
    za = _dot(nx, wa_ref[...])
    a_rate = _sigmoid(a0_ref[...] + _dot(za.astype(BF16), a2_ref[...]))
    ar_ref[...] = a_rate
    kkraw_ref[...] = zk * kk_ref[...]
    k_ref[...] = zk * (1.0 + (a_rate - 1.0) * ka_ref[...])


def _rk_call(nx, wr, wk, wa, a2, a0, kk, ka, tm):
    t, d2 = nx.shape
    w = wr.shape[1]
    la = wa.shape[1]
    out = jax.ShapeDtypeStruct((t, w), F32)
    return pl.pallas_call(
        _rk_kernel,
        grid=(t // tm,),
        in_specs=[_row_spec(tm, d2), _const_spec((d2, w)), _const_spec((d2, w)),
                  _const_spec((d2, la)), _const_spec((la, w)),
                  _const_spec((1, w)), _const_spec((1, w)), _const_spec((1, w))],
        out_specs=[_row_spec(tm, w)] * 4,
        out_shape=[out] * 4,
        compiler_params=_cparams("parallel"),
        name="rwkv_rk",
    )(nx, wr, wk, wa, a2, a0, kk, ka)


def _decay_from(nx, ww_ref, w2_ref, w0_ref):
    zw = _dot(nx, ww_ref[...])
    x = -(w0_ref[...] + _dot(jnp.tanh(zw).astype(BF16), w2_ref[...]))
    softplus = jnp.maximum(x, 0.0) + jnp.log(1.0 + jnp.exp(-jnp.abs(x)))
    return jnp.exp(-jnp.exp(-softplus - 0.5))


def _vw0_kernel(nx_ref, wv_ref, ww_ref, w2_ref, w0_ref, v_ref, dec_ref):
    nx = nx_ref[...]
    v_ref[...] = _dot(nx, wv_ref[...])
    dec_ref[...] = _decay_from(nx, ww_ref, w2_ref, w0_ref)


def _vw_kernel(nx_ref, wv_ref, ww_ref, w2_ref, w0_ref, m1_ref, m2_ref, v0_ref, vfirst_ref,
               v_ref, dec_ref):
    nx = nx_ref[...]
    zv = _dot(nx, wv_ref[...])
    lo = _dot(nx, m1_ref[...])
    vmix = _sigmoid(v0_ref[...] + _dot(lo.astype(BF16), m2_ref[...]))
    v_ref[...] = zv + (vfirst_ref[...] - zv) * vmix
    dec_ref[...] = _decay_from(nx, ww_ref, w2_ref, w0_ref)


def _vw_call(nx, wv, ww, w2, w0, mv, vfirst, tm):
    t, d2 = nx.shape
    w = wv.shape[1]
    lw = ww.shape[1]
    out = jax.ShapeDtypeStruct((t, w), F32)
    in_specs = [_row_spec(tm, d2), _const_spec((d2, w)), _const_spec((d2, lw)),
                _const_spec((lw, w)), _const_spec((1, w))]
    args = [nx, wv, ww, w2, w0]
    if mv is None:
        body = _vw0_kernel
    else:
        m1, m2, v0 = mv
        lm = m1.shape[1]
        body = _vw_kernel
        in_specs += [_const_spec((d2, lm)), _const_spec((lm, w)), _const_spec((1, w)),
                     _row_spec(tm, w)]
        args += [m1, m2, v0, vfirst]
    return pl.pallas_call(
        body,
        grid=(t // tm,),
        in_specs=in_specs,
        out_specs=[_row_spec(tm, w)] * 2,
        out_shape=[out] * 2,
        compiler_params=_cparams("parallel"),
        name="rwkv_vw",
    )(*args)


def _to_lanes(blk):
    r = jnp.concatenate([blk[s * 8:(s + 1) * 8, j * LANES:(j + 1) * LANES]
                         for s in range(2) for j in range(8)], axis=0)
    tr = r.T
    lo, hi = tr[0:64], tr[64:128]
    lane = lax.broadcasted_iota(jnp.int32, (64, LANES), 1)
    first = lane < 64
    return (jnp.where(first, lo, pltpu.roll(hi, 64, 1)),
            jnp.where(first, pltpu.roll(lo, 64, 1), hi))


def _from_lanes(o0, o1):
    lane = lax.broadcasted_iota(jnp.int32, (64, LANES), 1)
    first = lane < 64
    lo = jnp.where(first, o0, pltpu.roll(o1, 64, 1))
    hi = jnp.where(first, pltpu.roll(o0, 64, 1), o1)
    r = jnp.concatenate([lo, hi], axis=0).T
    rows = [jnp.concatenate([r[s * 64 + j * 8:s * 64 + j * 8 + 8] for j in range(8)], axis=1)
            for s in range(2)]
    return jnp.concatenate(rows, axis=0)


_WR, _WW, _WK, _WV, _WA, _WB = range(6)


def _wkv_kernel(r_ref, w_ref, k_ref, v_ref, kk_ref, ar_ref, rk_ref, lng_ref, lnb_ref,
                y_ref, st_ref, buf_a, buf_b, y_a, y_b, *, steps):
    n = RW_HEAD_DIM
    pairs = steps // 2

    @pl.when(pl.program_id(0) == 0)
    def _():
        st_ref[...] = jnp.zeros_like(st_ref)

    def load_pair(sp, buf):
        rows = pl.ds(pl.multiple_of(sp * 16, 16), 16)
        for idx, src in ((_WR, r_ref), (_WW, w_ref), (_WK, k_ref), (_WV, v_ref)):
            o0, o1 = _to_lanes(src[rows, :])
            buf[idx, 0] = o0
            buf[idx, 1] = o1
        kk0, kk1 = _to_lanes(kk_ref[rows, :])
        ar0, ar1 = _to_lanes(ar_ref[rows, :])
        for off, kk, ar in ((0, kk0, ar0), (1, kk1, ar1)):
            ss = jnp.sum(kk * kk, axis=0, keepdims=True)
            kkn = kk * lax.rsqrt(jnp.maximum(ss, 1e-24))
            buf[_WA, off] = -kkn
            buf[_WB, off] = kkn * ar

    def step(buf, off, nbuf, noff, ybuf, sa):
        nxt = []
        hv = n // 2
        for half in range(2):
            vs = slice(half * hv, (half + 1) * hv)
            sa_h = sa[vs]
            vv = buf[_WV, off, vs, :]
            y = jnp.zeros((hv, LANES), F32)
            san = jnp.zeros((hv, LANES), F32)
            for k in range(n):
                row = slice(k, k + 1)
                sk = (st_ref[k, vs, :] * buf[_WW, off, row, :] + sa_h * buf[_WB, off, row, :]
                      + vv * buf[_WK, off, row, :])
                st_ref[k, vs, :] = sk
                y = y + sk * buf[_WR, off, row, :]
                san = san + sk * nbuf[_WA, noff, row, :]
            ybuf[off, vs, :] = y
            nxt.append(san)
        return jnp.concatenate(nxt, axis=0)

    rk = rk_ref[...]
    lng = lng_ref[...]
    lnb = lnb_ref[...]

    def finish(buf, off, ybuf):
        y = ybuf[off]
        mu = jnp.mean(y, axis=0, keepdims=True)
        yc = y - mu
        var = jnp.mean(yc * yc, axis=0, keepdims=True)
        yn = yc * lax.rsqrt(var + RW_LN_EPS) * lng + lnb
        bonus = jnp.sum(buf[_WR, off] * buf[_WK, off] * rk, axis=0, keepdims=True)
        return yn + bonus * buf[_WV, off]

    def store_pair(sp, buf, ybuf):
        rows = pl.ds(pl.multiple_of(sp * 16, 16), 16)
        y_ref[rows, :] = _from_lanes(finish(buf, 0, ybuf), finish(buf, 1, ybuf))

    load_pair(0, buf_a)
    sa0 = jnp.zeros((n, LANES), F32)
    for k in range(n):
        sa0 = sa0 + st_ref[k] * buf_a[_WA, 0, k:k + 1, :]

    def two_pairs(q, sa):
        sa = step(buf_a, 0, buf_a, 1, y_a, sa)
        load_pair(2 * q + 1, buf_b)
        sa = step(buf_a, 1, buf_b, 0, y_a, sa)
        sa = step(buf_b, 0, buf_b, 1, y_b, sa)
        store_pair(2 * q, buf_a, y_a)
        load_pair(jnp.minimum(2 * q + 2, pairs - 1), buf_a)
        sa = step(buf_b, 1, buf_a, 0, y_b, sa)
        store_pair(2 * q + 1, buf_b, y_b)
        return sa

    lax.fori_loop(0, pairs // 2, two_pairs, sa0)


def _wkv_call(r, w, k, v, kkraw, ar, rk_l, lng_l, lnb_l, batch, steps):
    t, width = r.shape
    rows = steps * batch
    n = RW_HEAD_DIM
    assert steps % 4 == 0
    pair_buf = pltpu.VMEM((6, 2, n, LANES), F32)
    pair_out = pltpu.VMEM((2, n, LANES), F32)
    return pl.pallas_call(
        functools.partial(_wkv_kernel, steps=steps),
        grid=(t // rows,),
        in_specs=[_row_spec(rows, width)] * 6 + [_const_spec((n, LANES))] * 3,
        out_specs=_row_spec(rows, width),
        out_shape=jax.ShapeDtypeStruct((t, width), F32),
        scratch_shapes=[pltpu.VMEM((n, n, LANES), F32), pair_buf, pair_buf, pair_out, pair_out],
        compiler_params=_cparams("arbitrary"),
        name="wkv7_scan",
    )(r, w, k, v, kkraw, ar, rk_l, lng_l, lnb_l)


def _head_param_lanes(pv, batch):
    a = pv.reshape(RW_HEADS // 2, 2, RW_HEAD_DIM).transpose(2, 1, 0)
    a = jnp.broadcast_to(a[..., None], a.shape + (batch,))
    return a.reshape(RW_HEAD_DIM, LANES).astype(F32)


def _gmlp_kernel(nx_ref, wu_ref, wg_ref, lng_ref, lnb_ref, ws_ref, bias_ref, yb_ref):
    nx = nx_ref[...]
    u = _gelu(_dot(nx, wu_ref[...]))
    gv = _gelu(_dot(nx, wg_ref[...]))
    mu = jnp.mean(gv, axis=-1, keepdims=True)
    gc = gv - mu
    var = jnp.mean(gc * gc, axis=-1, keepdims=True)
    v = (gc * lax.rsqrt(var + GM_LN_EPS) * lng_ref[...] + lnb_ref[...]).astype(BF16)
    gd = v.shape[1] // GM_GROUPS
    for g in range(GM_GROUPS):
        cols = slice(g * gd, (g + 1) * gd)
        s = _dot(ws_ref[g], v[:, cols]) + bias_ref[:, g:g + 1]
        yb_ref[:, cols] = (u[:, cols] * s).astype(BF16)


def _gmlp_call(nx, wu, wg, lng, lnb, ws_k, bias):
    t, d2 = nx.shape
    w = wu.shape[1]
    rows = ws_k.shape[1]
    return pl.pallas_call(
        _gmlp_kernel,
        grid=(t // rows,),
        in_specs=[_row_spec(rows, d2), _const_spec((d2, w)), _const_spec((d2, w)),
                  _const_spec((1, w)), _const_spec((1, w)),
                  _const_spec(ws_k.shape), _const_spec((rows, GM_GROUPS))],
        out_specs=_row_spec(rows, w),
        out_shape=jax.ShapeDtypeStruct((t, w), BF16),
        compiler_params=_cparams("parallel"),
        name="gmlp",
    )(nx, wu, wg, lng, lnb, ws_k, bias)


def _merge_kernel(nx_ref, y_ref, yb_ref, h_ref, wg_ref, g2_ref, wga_ref, wgb_ref,
                  wa_ref, wb_ref, wo_ref, nf_ref, hn_ref, n2_ref):
    nx = nx_ref[...]
    gate = _dot(_sigmoid(_dot(nx, wg_ref[...])).astype(BF16), g2_ref[...])
    ya = (y_ref[...] * gate).astype(BF16)
    pa = _dot(ya, wa_ref[...])
    pb = _dot(yb_ref[...], wb_ref[...])
    merged = _sigmoid(_dot(nx, wga_ref[...])) * pa + _sigmoid(_dot(nx, wgb_ref[...])) * pb
    hn = h_ref[...] + _dot(merged.astype(BF16), wo_ref[...])
    hn_ref[...] = hn
    n2_ref[...] = _rms(hn, nf_ref[...]).astype(BF16)


def _merge_call(nx, y, yb, h, wg, g2, wga, wgb, wa, wb, wo, nf, tm):
    t, d2 = nx.shape
    d = h.shape[1]
    w = y.shape[1]
    lg = wg.shape[1]
    return pl.pallas_call(
        _merge_kernel,
        grid=(t // tm,),
        in_specs=[_row_spec(tm, d2), _row_spec(tm, w), _row_spec(tm, w), _row_spec(tm, d),
                  _const_spec((d2, lg)), _const_spec((lg, w)),
                  _const_spec((d2, d)), _const_spec((d2, d)),
                  _const_spec((w, d)), _const_spec((w, d)), _const_spec((d, d)),
                  _const_spec((1, d))],
        out_specs=[_row_spec(tm, d), _row_spec(tm, d)],
        out_shape=[jax.ShapeDtypeStruct((t, d), F32), jax.ShapeDtypeStruct((t, d), BF16)],
        compiler_params=_cparams("parallel"),
        name="merge",
    )(nx, y, yb, h, wg, g2, wga, wgb, wa, wb, wo, nf)


def _top_values(s, count, rows):
    tops = []
    for _ in range(count):
        m = jnp.max(s, axis=0, keepdims=True)
        tops.append(m)
        s = jnp.where(s == m, -jnp.inf, s)
    tops += [jnp.full_like(tops[0], -jnp.inf)] * (rows - count)
    return jnp.concatenate(tops, axis=0)


def _peer_select_kernel(n2_ref, h_ref, p_ref, wq_ref, keys_ref, wple_ref, wpg_ref,
                        thr_ref, s2_ref, e1_ref, e2_ref, hp_ref, q_scr):
    n2 = n2_ref[...]
    ple = _dot(p_ref[...], wple_ref[...]) * _sigmoid(_dot(n2, wpg_ref[...]))
    hp_ref[...] = h_ref[...] + ple
    q = _dot(n2, wq_ref[...]).astype(BF16)
    for hp in range(2 * PEER_HEADS):
        q_scr[hp] = q[:, hp * PEER_HALF:(hp + 1) * PEER_HALF]
    tt = n2.shape[0]
    kk = PEER_TOPK
    nt = (((1,), (1,)), ((), ()))

    def head(h, _):
        s1 = lax.dot_general(keys_ref[h, 0], q_scr[2 * h], nt, preferred_element_type=F32)
        s2 = lax.dot_general(keys_ref[h, 1], q_scr[2 * h + 1], nt, preferred_element_type=F32)
        def chunk_stats(c):
            cols = slice(c * LANES, (c + 1) * LANES)
            t1 = _top_values(s1[:, cols], kk + 1, 24)
            t2 = _top_values(s2[:, cols], kk + 1, 24)
            cand = [t1[0:1] + t2]
            cand += [t1[i:i + 1] + t2[0:8] for i in range(1, 8)]
            cand += [t1[8:24] + t2[0:1]]
            best = _top_values(jnp.concatenate(cand, axis=0), kk + 1, kk + 1)
            top = best[0:kk]
            inv_z = 1.0 / jnp.sum(jnp.exp(top - top[0:1]), axis=0, keepdims=True)
            tau = 0.5 * (best[kk - 1:kk] + best[kk:kk + 1])
            return tau, t1[0:1], t2[0:1], inv_z

        group = 2
        per = PEER_CHUNK // LANES
        for c0 in range(0, tt // LANES, group):
            stats = [chunk_stats(c) for c in range(c0, c0 + group)]
            for c, (tau, m1, m2, inv_z) in zip(range(c0, c0 + group), stats):
                cols = slice(c * LANES, (c + 1) * LANES)
                dst = (h, c // per, slice(None), slice((c % per) * LANES, (c % per + 1) * LANES))
                s2_ref[dst] = s2[:, cols]
                thr_ref[dst] = tau - s1[:, cols]
                e1_ref[dst] = jnp.exp(s1[:, cols] - m1)
                e2_ref[dst] = jnp.exp(s2[:, cols] - m2) * inv_z
        return 0

    lax.fori_loop(0, PEER_HEADS, head, 0)


def _peer_select_call(n2, h, p, wq, keys, wple, wpg, tt):
    t, d = n2.shape
    pd = p.shape[1]
    dq = wq.shape[1]
    nh, nk = PEER_HEADS, PEER_KEYS
    tspec = pl.BlockSpec((nh, tt // PEER_CHUNK, nk, PEER_CHUNK), lambda i: (0, i, 0, 0))
    tshape = jax.ShapeDtypeStruct((nh, t // PEER_CHUNK, nk, PEER_CHUNK), F32)
    return pl.pallas_call(
        _peer_select_kernel,
        grid=(t // tt,),
        in_specs=[_row_spec(tt, d), _row_spec(tt, d), _row_spec(tt, pd),
                  _const_spec((d, dq)), _const_spec(keys.shape),
                  _const_spec((pd, d)), _const_spec((d, d))],
        out_specs=[tspec] * 4 + [_row_spec(tt, d)],
        out_shape=[tshape] * 4 + [jax.ShapeDtypeStruct((t, d), F32)],
        scratch_shapes=[pltpu.VMEM((2 * nh, tt, PEER_HALF), BF16)],
        compiler_params=_cparams("parallel"),
        name="peer_select",
    )(n2, h, p, wq, keys, wple, wpg)


def _peer_dense_kernel(n2_ref, u_ref, vt_ref, thr_ref, e1_ref, s2_ref, e2_ref, hp_ref,
                       out_ref, acc_ref, act0_ref, act1_ref, ga0_ref, ga1_ref, *, nj):
    f = pl.program_id(0)
    nt = (((1,), (1,)), ((), ()))
    nc, te, cw = act0_ref.shape
    nk = PEER_KEYS
    rb = 2 * SUBLANES
    j3 = jnp.maximum(f - 2, 0) % nj

    @pl.when(f == 0)
    def _():
        act1_ref[...] = jnp.zeros_like(act1_ref)
        ga0_ref[...] = jnp.zeros_like(ga0_ref)

    @pl.when(j3 == 0)
    def _():
        acc_ref[...] = jnp.zeros_like(acc_ref)

    d = u_ref.shape[1]
    ks = 2 * LANES

    def stages(act_new, act_cur, ga_cur, ga_old):
        for c in range(nc):
            tok = slice(c * cw, (c + 1) * cw)

            def gate_block(i1):
                thr = [thr_ref[h, c, i1:i1 + 1, :] for h in range(PEER_HEADS)]
                e1 = [e1_ref[h, c, i1:i1 + 1, :] for h in range(PEER_HEADS)]
                for r in range(nk // rb):
                    rs = slice(r * rb, (r + 1) * rb)
                    g = None
                    for h in range(PEER_HEADS):
                        keep = s2_ref[h, c, rs, :] >= thr[h]
                        term = jnp.where(keep, e2_ref[h, c, rs, :], 0.0) * e1[h]
                        g = term if g is None else g + term
                    rows = slice(i1 * nk + r * rb, i1 * nk + (r + 1) * rb)
                    ga_cur[c, rows, :] = (g * _gelu(act_cur[c, rows, :])).astype(BF16)

            def act_slice(q):
                kq = slice(q * ks, (q + 1) * ks)
                part = lax.dot_general(u_ref[:, kq], n2_ref[tok, kq], nt,
                                       preferred_element_type=F32)
                if q == 0:
                    act_new[c] = part
                else:
                    act_new[c] += part

            def out_slice(q):
                kq = slice(q * ks, (q + 1) * ks)
                acc_ref[c] += _dot(vt_ref[:, kq], ga_old[c, kq, :])

            mxu_work = [functools.partial(fn, q) for q in range(max(d, te) // ks)
                        for fn in (act_slice, out_slice)
                        if q < (d if fn is act_slice else te) // ks]
            blocks = te // nk
            for i1 in range(blocks):
                gate_block(i1)
                lo = i1 * len(mxu_work) // blocks
                hi = (i1 + 1) * len(mxu_work) // blocks
                for work in mxu_work[lo:hi]:
                    work()

    @pl.when(f % 2 == 0)
    def _():
        stages(act0_ref, act1_ref, ga1_ref, ga0_ref)

    @pl.when(f % 2 == 1)
    def _():
        stages(act1_ref, act0_ref, ga0_ref, ga1_ref)

    @pl.when(jnp.logical_and(f >= 2, j3 == nj - 1))
    def _():
        for c in range(nc):
            out_ref[c * cw:(c + 1) * cw, :] = hp_ref[c * cw:(c + 1) * cw, :] + acc_ref[c].T


def _peer_dense_call(n2, u, vt, thr, e1, s2, e2, hp, tt, te):
    t, d = n2.shape
    ne = u.shape[0]
    nh, nk = PEER_HEADS, PEER_KEYS
    r1 = te // nk
    cw = PEER_CHUNK
    nc = tt // cw
    ni, nj = t // tt, ne // te
    last = ni * nj - 1

    def pair(f, lag):
        a = jnp.clip(f - lag, 0, last)
        return a // nj, a % nj

    score_blk = (nh, nc, nk, cw)
    row_blk = (nh, nc, r1, cw)
    return pl.pallas_call(
        functools.partial(_peer_dense_kernel, nj=nj),
        grid=(ni * nj + 2,),
        in_specs=[pl.BlockSpec((tt, d), lambda f: (pair(f, 0)[0], 0)),
                  pl.BlockSpec((te, d), lambda f: (pair(f, 0)[1], 0)),
                  pl.BlockSpec((d, te), lambda f: (0, pair(f, 2)[1])),
                  pl.BlockSpec(row_blk, lambda f: (0, pair(f, 1)[0], pair(f, 1)[1], 0)),
                  pl.BlockSpec(row_blk, lambda f: (0, pair(f, 1)[0], pair(f, 1)[1], 0)),
                  pl.BlockSpec(score_blk, lambda f: (0, pair(f, 1)[0], 0, 0)),
                  pl.BlockSpec(score_blk, lambda f: (0, pair(f, 1)[0], 0, 0)),
                  pl.BlockSpec((tt, d), lambda f: (pair(f, 2)[0], 0))],
        out_specs=pl.BlockSpec((tt, d), lambda f: (pair(f, 2)[0], 0)),
        out_shape=jax.ShapeDtypeStruct((t, d), F32),
        scratch_shapes=[pltpu.VMEM((nc, d, cw), F32)]
        + [pltpu.VMEM((nc, te, cw), F32)] * 2 + [pltpu.VMEM((nc, te, cw), BF16)] * 2,
        compiler_params=_cparams("arbitrary"),
        name="peer_dense",
    )(n2, u, vt, thr, e1, s2, e2, hp)


def _final_norm_kernel(h_ref, g_ref, o_ref):
    o_ref[0] = _rms(h_ref[...], g_ref[...])


def _final_norm_call(h2d, g, batch, seq, ts):
    d = g.shape[-1]
    return pl.pallas_call(
        _final_norm_kernel,
        grid=(seq // ts, batch),
        in_specs=[pl.BlockSpec((ts, d), lambda s, b: (s, b)), _const_spec((1, d))],
        out_specs=pl.BlockSpec((1, ts, d), lambda s, b: (b, s, 0)),
        out_shape=jax.ShapeDtypeStruct((batch, seq, d), F32),
        compiler_params=_cparams("parallel", "parallel"),
        name="final_norm",
    )(h2d, g.reshape(1, d))


def _tile(total, want):
    want = min(want, total)
    assert total % want == 0, (total, want)
    return want


def kernel(x, p, norm_mix, norm_ffn, norm_final, w_in, rw_w0, rw_w2, rw_a0, rw_a2, rw_g2, rw_kk, rw_ka, rw_rk, rw_ln_g, rw_ln_b, rw_mv_w1, rw_mv_w2, rw_mv_v0, gm_ln_g, gm_ln_b, gm_ws, gm_bs, w_proj_a, w_proj_b, w_out, peer_wq, peer_keys, peer_u, peer_v, ple_w, ple_gate):
    batch, seq, d = x.shape
    depth = w_in.shape[0]
    t = batch * seq
    w = RW_HEADS * RW_HEAD_DIM
    assert batch * RW_HEADS == LANES and batch == SUBLANES and d == w
    assert seq % GM_CHUNK == 0

    tm = _tile(t, 512)
    tt = _tile(t, 512)
    te = 1024
    steps = _tile(seq, 64)

    gw = gm_ln_g.shape[1]
    sizes = (w, w, w, rw_w2.shape[1], rw_a2.shape[1], rw_g2.shape[1], gw, gw, d, d)
    offs = [0]
    for sz in sizes:
        offs.append(offs[-1] + sz)

    def col(wi, idx):
        return wi[:, offs[idx]:offs[idx + 1]].astype(BF16)

    h = x.transpose(1, 0, 2).reshape(t, d)
    pt = p.transpose(0, 2, 1, 3).reshape(depth, t, p.shape[-1]).astype(BF16)
    row = lambda a: a.reshape(1, -1).astype(F32)
    expand = jnp.repeat(jnp.eye(GM_CHUNK, dtype=F32), batch, axis=0)
    same_batch = jnp.tile(jnp.eye(batch, dtype=F32), (GM_CHUNK, GM_CHUNK))
    tril = jnp.tril(jnp.ones((GM_CHUNK, GM_CHUNK), F32))

    v_first = None
    for i in range(depth):
        wi = w_in[i]
        nx = _norm_shift(h, norm_mix[i], batch, tm)

        r, k, kkraw, ar = _rk_call(nx, col(wi, 0), col(wi, 1), col(wi, 4), rw_a2[i].astype(BF16),
                                   row(rw_a0[i]), row(rw_kk[i]), row(rw_ka[i]), tm)
        mv = None
        if i > 0:
            mv = (rw_mv_w1[i - 1].astype(BF16), rw_mv_w2[i - 1].astype(BF16), row(rw_mv_v0[i - 1]))
        v, dec = _vw_call(nx, col(wi, 2), col(wi, 3), rw_w2[i].astype(BF16), row(rw_w0[i]), mv,
                          v_first, tm)
        if i == 0:
            v_first = v

        y = _wkv_call(r, dec, k, v, kkraw, ar,
                      _head_param_lanes(rw_rk[i], batch),
                      _head_param_lanes(rw_ln_g[i].reshape(RW_HEADS, RW_HEAD_DIM), batch),
                      _head_param_lanes(rw_ln_b[i].reshape(RW_HEADS, RW_HEAD_DIM), batch),
                      batch, steps)

        ws_k = (jnp.einsum("ri,gij,cj->grc", expand, gm_ws[i] * tril, expand)
                * same_batch).astype(BF16)
        bias = jnp.repeat(gm_bs[i].T, batch, axis=0).astype(F32)
        yb = _gmlp_call(nx, col(wi, 6), col(wi, 7), row(gm_ln_g[i]), row(gm_ln_b[i]), ws_k, bias)

        h, n2 = _merge_call(nx, y, yb, h, col(wi, 5), rw_g2[i].astype(BF16), col(wi, 8), col(wi, 9),
                            w_proj_a[i].astype(BF16), w_proj_b[i].astype(BF16),
                            w_out[i].astype(BF16), row(norm_ffn[i]), tm)

        thr, s2, e1, e2, hp = _peer_select_call(
            n2, h, pt[i], peer_wq[i].astype(BF16), peer_keys[i].astype(BF16),
            ple_w[i].astype(BF16), ple_gate[i].astype(BF16), tt)
        h = _peer_dense_call(n2, peer_u[i].astype(BF16), peer_v[i].T.astype(BF16),
                             thr, e1, s2, e2, hp, tt, te)

    return _final_norm_call(h.reshape(seq, batch * d), norm_final, batch, seq, _tile(seq, 512))
```

```python
import functools
import math

import jax
import jax.numpy as jnp
from jax import lax
from jax.experimental import pallas as pl
from jax.experimental.pallas import tpu as pltpu

F32 = jnp.float32
BF16 = jnp.bfloat16

RW_HEADS = 16
RW_HEAD_DIM = 64
RW_LN_EPS = 64e-5
GM_GROUPS = 8
GM_CHUNK = 128
GM_LN_EPS = 1e-5
PEER_HEADS = 8
PEER_KEYS = 128
PEER_HALF = 128
PEER_TOPK = 16
PEER_CHUNK = 256
RMS_EPS = 1e-6

LANES = 128
SUBLANES = 8
VMEM_LIMIT = 56 * 1024 * 1024


def _cparams(*sem):
    return pltpu.CompilerParams(dimension_semantics=tuple(sem), vmem_limit_bytes=VMEM_LIMIT)


def _const_spec(shape):
    nd = len(shape)
    return pl.BlockSpec(shape, lambda *_: (0,) * nd, pipeline_mode=pl.Buffered(1))


def _row_spec(rows, cols):
    return pl.BlockSpec((rows, cols), lambda i: (i, 0))


def _gelu(x):
    c = math.sqrt(2.0 / math.pi)
    return 0.5 * x * (1.0 + jnp.tanh(c * (x + 0.044715 * (x * x * x))))


def _gelu_x2(x):
    c = math.sqrt(2.0 / math.pi)
    return x * (1.0 + jnp.tanh(x * (c + (0.044715 * c) * (x * x))))


def _sigmoid(x):
    return 1.0 / (1.0 + jnp.exp(-x))


def _dot(a, b):
    return jnp.dot(a, b, preferred_element_type=F32)


def _rms(x, g):
    return x * lax.rsqrt(jnp.mean(x * x, axis=-1, keepdims=True) + RMS_EPS) * g


def _norm_shift_kernel(h_ref, hprev_ref, g_ref, nx_ref, *, batch):
    i = pl.program_id(0)
    g = g_ref[...]
    n = _rms(h_ref[...], g)
    prev = _rms(hprev_ref[...], g) * (i > 0).astype(F32)
    d = n.shape[1]
    shifted = jnp.concatenate([prev, n[:-batch]], axis=0)
    nx_ref[:, :d] = n.astype(BF16)
    nx_ref[:, d:] = shifted.astype(BF16)


def _norm_shift(h, g, batch, tm):
    t, d = h.shape
    per = tm // batch
    return pl.pallas_call(
        functools.partial(_norm_shift_kernel, batch=batch),
        grid=(t // tm,),
        in_specs=[_row_spec(tm, d),
                  pl.BlockSpec((batch, d), lambda i: (jnp.maximum(i * per - 1, 0), 0)),
                  _const_spec((1, d))],
        out_specs=_row_spec(tm, 2 * d),
        out_shape=jax.ShapeDtypeStruct((t, 2 * d), BF16),
        compiler_params=_cparams("parallel"),
        name="norm_shift",
    )(h, h, g.reshape(1, d))


def _rk_kernel(nx_ref, wr_ref, wk_ref, wa_ref, a2_ref, a0_ref, kk_ref, ka_ref,
               r_ref, k_ref, kkraw_ref, ar_ref):
    nx = nx_ref[...]
    r_ref[...] = _dot(nx, wr_ref[...])
    zk = _dot(nx, wk_ref[...])
    za = _dot(nx, wa_ref[...])
    a_rate = _sigmoid(a0_ref[...] + _dot(za.astype(BF16), a2_ref[...]))
    ar_ref[...] = a_rate
    kkraw_ref[...] = zk * kk_ref[...]
    k_ref[...] = zk * (1.0 + (a_rate - 1.0) * ka_ref[...])


def _rk_call(nx, wr, wk, wa, a2, a0, kk, ka, tm):
    t, d2 = nx.shape
    w = wr.shape[1]
    la = wa.shape[1]
    out = jax.ShapeDtypeStruct((t, w), F32)
    return pl.pallas_call(
        _rk_kernel,
        grid=(t // tm,),
        in_specs=[_row_spec(tm, d2), _const_spec((d2, w)), _const_spec((d2, w)),
                  _const_spec((d2, la)), _const_spec((la, w)),
                  _const_spec((1, w)), _const_spec((1, w)), _const_spec((1, w))],
        out_specs=[_row_spec(tm, w)] * 4,
        out_shape=[out] * 4,
        compiler_params=_cparams("parallel"),
        name="rwkv_rk",
    )(nx, wr, wk, wa, a2, a0, kk, ka)


def _decay_from(nx, ww_ref, w2_ref, w0_ref):
    zw = _dot(nx, ww_ref[...])
    x = -(w0_ref[...] + _dot(jnp.tanh(zw).astype(BF16), w2_ref[...]))
    softplus = jnp.maximum(x, 0.0) + jnp.log(1.0 + jnp.exp(-jnp.abs(x)))
    return jnp.exp(-jnp.exp(-softplus - 0.5))


def _vw0_kernel(nx_ref, wv_ref, ww_ref, w2_ref, w0_ref, v_ref, dec_ref):
    nx = nx_ref[...]
    v_ref[...] = _dot(nx, wv_ref[...])
    dec_ref[...] = _decay_from(nx, ww_ref, w2_ref, w0_ref)


def _vw_kernel(nx_ref, wv_ref, ww_ref, w2_ref, w0_ref, m1_ref, m2_ref, v0_ref, vfirst_ref,
               v_ref, dec_ref):
    nx = nx_ref[...]
    zv = _dot(nx, wv_ref[...])
    lo = _dot(nx, m1_ref[...])
    vmix = _sigmoid(v0_ref[...] + _dot(lo.astype(BF16), m2_ref[...]))
    v_ref[...] = zv + (vfirst_ref[...] - zv) * vmix
    dec_ref[...] = _decay_from(nx, ww_ref, w2_ref, w0_ref)


def _vw_call(nx, wv, ww, w2, w0, mv, vfirst, tm):
    t, d2 = nx.shape
    w = wv.shape[1]
    lw = ww.shape[1]
    out = jax.ShapeDtypeStruct((t, w), F32)
    in_specs = [_row_spec(tm, d2), _const_spec((d2, w)), _const_spec((d2, lw)),
                _const_spec((lw, w)), _const_spec((1, w))]
    args = [nx, wv, ww, w2, w0]
    if mv is None:
        body = _vw0_kernel
    else:
        m1, m2, v0 = mv
        lm = m1.shape[1]
        body = _vw_kernel
        in_specs += [_const_spec((d2, lm)), _const_spec((lm, w)), _const_spec((1, w)),
                     _row_spec(tm, w)]
        args += [m1, m2, v0, vfirst]
    return pl.pallas_call(
        body,
        grid=(t // tm,),
        in_specs=in_specs,
        out_specs=[_row_spec(tm, w)] * 2,
        out_shape=[out] * 2,
        compiler_params=_cparams("parallel"),
        name="rwkv_vw",
    )(*args)


def _to_lanes(blk):
    r = jnp.concatenate([blk[s * 8:(s + 1) * 8, j * LANES:(j + 1) * LANES]
                         for s in range(2) for j in range(8)], axis=0)
    tr = r.T
    lo, hi = tr[0:64], tr[64:128]
    lane = lax.broadcasted_iota(jnp.int32, (64, LANES), 1)
    first = lane < 64
    return (jnp.where(first, lo, pltpu.roll(hi, 64, 1)),
            jnp.where(first, pltpu.roll(lo, 64, 1), hi))


def _from_lanes(o0, o1):
    lane = lax.broadcasted_iota(jnp.int32, (64, LANES), 1)
    first = lane < 64
    lo = jnp.where(first, o0, pltpu.roll(o1, 64, 1))
    hi = jnp.where(first, pltpu.roll(o0, 64, 1), o1)
    r = jnp.concatenate([lo, hi], axis=0).T
    rows = [jnp.concatenate([r[s * 64 + j * 8:s * 64 + j * 8 + 8] for j in range(8)], axis=1)
            for s in range(2)]
    return jnp.concatenate(rows, axis=0)


_WR, _WW, _WK, _WV, _WA, _WB = range(6)


def _wkv_kernel(r_ref, w_ref, k_ref, v_ref, kk_ref, ar_ref, rk_ref, lng_ref, lnb_ref,
                y_ref, st_ref, buf_a, buf_b, y_a, y_b, *, steps):
    n = RW_HEAD_DIM
    pairs = steps // 2

    @pl.when(pl.program_id(0) == 0)
    def _():
        st_ref[...] = jnp.zeros_like(st_ref)

    def load_pair(sp, buf):
        rows = pl.ds(pl.multiple_of(sp * 16, 16), 16)
        for idx, src in ((_WR, r_ref), (_WW, w_ref), (_WK, k_ref), (_WV, v_ref)):
            o0, o1 = _to_lanes(src[rows, :])
            buf[idx, 0] = o0
            buf[idx, 1] = o1
        kk0, kk1 = _to_lanes(kk_ref[rows, :])
        ar0, ar1 = _to_lanes(ar_ref[rows, :])
        for off, kk, ar in ((0, kk0, ar0), (1, kk1, ar1)):
            ss = jnp.sum(kk * kk, axis=0, keepdims=True)
            kkn = kk * lax.rsqrt(jnp.maximum(ss, 1e-24))
            buf[_WA, off] = -kkn
            buf[_WB, off] = kkn * ar

    def step(buf, off, nbuf, noff, ybuf, sa):
        nxt = []
        hv = n // 2
        for half in range(2):
            vs = slice(half * hv, (half + 1) * hv)
            sa_h = sa[vs]
            vv = buf[_WV, off, vs, :]
            y = jnp.zeros((hv, LANES), F32)
            san = jnp.zeros((hv, LANES), F32)
            for k in range(n):
                row = slice(k, k + 1)
                sk = (st_ref[k, vs, :] * buf[_WW, off, row, :] + sa_h * buf[_WB, off, row, :]
                      + vv * buf[_WK, off, row, :])
                st_ref[k, vs, :] = sk
                y = y + sk * buf[_WR, off, row, :]
                san = san + sk * nbuf[_WA, noff, row, :]
            ybuf[off, vs, :] = y
            nxt.append(san)
        return jnp.concatenate(nxt, axis=0)

    rk = rk_ref[...]
    lng = lng_ref[...]
    lnb = lnb_ref[...]

    def finish(buf, off, ybuf):
        y = ybuf[off]
        mu = jnp.mean(y, axis=0, keepdims=True)
        yc = y - mu
        var = jnp.mean(yc * yc, axis=0, keepdims=True)
        yn = yc * lax.rsqrt(var + RW_LN_EPS) * lng + lnb
        bonus = jnp.sum(buf[_WR, off] * buf[_WK, off] * rk, axis=0, keepdims=True)
        return yn + bonus * buf[_WV, off]

    def store_pair(sp, buf, ybuf):
        rows = pl.ds(pl.multiple_of(sp * 16, 16), 16)
        y_ref[rows, :] = _from_lanes(finish(buf, 0, ybuf), finish(buf, 1, ybuf))

    load_pair(0, buf_a)
    sa0 = jnp.zeros((n, LANES), F32)
    for k in range(n):
        sa0 = sa0 + st_ref[k] * buf_a[_WA, 0, k:k + 1, :]

    def two_pairs(q, sa):
        sa = step(buf_a, 0, buf_a, 1, y_a, sa)
        load_pair(2 * q + 1, buf_b)
        sa = step(buf_a, 1, buf_b, 0, y_a, sa)
        sa = step(buf_b, 0, buf_b, 1, y_b, sa)
        store_pair(2 * q, buf_a, y_a)
        load_pair(jnp.minimum(2 * q + 2, pairs - 1), buf_a)
        sa = step(buf_b, 1, buf_a, 0, y_b, sa)
        store_pair(2 * q + 1, buf_b, y_b)
        return sa

    lax.fori_loop(0, pairs // 2, two_pairs, sa0)


def _wkv_call(r, w, k, v, kkraw, ar, rk_l, lng_l, lnb_l, batch, steps):
    t, width = r.shape
    rows = steps * batch
    n = RW_HEAD_DIM
    assert steps % 4 == 0
    pair_buf = pltpu.VMEM((6, 2, n, LANES), F32)
    pair_out = pltpu.VMEM((2, n, LANES), F32)
    return pl.pallas_call(
        functools.partial(_wkv_kernel, steps=steps),
        grid=(t // rows,),
        in_specs=[_row_spec(rows, width)] * 6 + [_const_spec((n, LANES))] * 3,
        out_specs=_row_spec(rows, width),
        out_shape=jax.ShapeDtypeStruct((t, width), F32),
        scratch_shapes=[pltpu.VMEM((n, n, LANES), F32), pair_buf, pair_buf, pair_out, pair_out],
        compiler_params=_cparams("arbitrary"),
        name="wkv7_scan",
    )(r, w, k, v, kkraw, ar, rk_l, lng_l, lnb_l)


def _head_param_lanes(pv, batch):
    a = pv.reshape(RW_HEADS // 2, 2, RW_HEAD_DIM).transpose(2, 1, 0)
    a = jnp.broadcast_to(a[..., None], a.shape + (batch,))
    return a.reshape(RW_HEAD_DIM, LANES).astype(F32)


def _gmlp_kernel(nx_ref, wu_ref, wg_ref, lng_ref, lnb_ref, ws_ref, bias_ref, yb_ref):
    nx = nx_ref[...]
    u = _gelu(_dot(nx, wu_ref[...]))
    gv = _gelu(_dot(nx, wg_ref[...]))
    mu = jnp.mean(gv, axis=-1, keepdims=True)
    gc = gv - mu
    var = jnp.mean(gc * gc, axis=-1, keepdims=True)
    v = (gc * lax.rsqrt(var + GM_LN_EPS) * lng_ref[...] + lnb_ref[...]).astype(BF16)
    gd = v.shape[1] // GM_GROUPS
    for g in range(GM_GROUPS):
        cols = slice(g * gd, (g + 1) * gd)
        s = _dot(ws_ref[g], v[:, cols]) + bias_ref[:, g:g + 1]
        yb_ref[:, cols] = (u[:, cols] * s).astype(BF16)


def _gmlp_call(nx, wu, wg, lng, lnb, ws_k, bias):
    t, d2 = nx.shape
    w = wu.shape[1]
    rows = ws_k.shape[1]
    return pl.pallas_call(
        _gmlp_kernel,
        grid=(t // rows,),
        in_specs=[_row_spec(rows, d2), _const_spec((d2, w)), _const_spec((d2, w)),
                  _const_spec((1, w)), _const_spec((1, w)),
                  _const_spec(ws_k.shape), _const_spec((rows, GM_GROUPS))],
        out_specs=_row_spec(rows, w),
        out_shape=jax.ShapeDtypeStruct((t, w), BF16),
        compiler_params=_cparams("parallel"),
        name="gmlp",
    )(nx, wu, wg, lng, lnb, ws_k, bias)


def _merge_kernel(nx_ref, y_ref, yb_ref, h_ref, wg_ref, g2_ref, wga_ref, wgb_ref,
                  wa_ref, wb_ref, wo_ref, nf_ref, hn_ref, n2_ref):
    nx = nx_ref[...]
    gate = _dot(_sigmoid(_dot(nx, wg_ref[...])).astype(BF16), g2_ref[...])
    ya = (y_ref[...] * gate).astype(BF16)
    pa = _dot(ya, wa_ref[...])
    pb = _dot(yb_ref[...], wb_ref[...])
    merged = _sigmoid(_dot(nx, wga_ref[...])) * pa + _sigmoid(_dot(nx, wgb_ref[...])) * pb
    hn = h_ref[...] + _dot(merged.astype(BF16), wo_ref[...])
    hn_ref[...] = hn
    n2_ref[...] = _rms(hn, nf_ref[...]).astype(BF16)


def _merge_call(nx, y, yb, h, wg, g2, wga, wgb, wa, wb, wo, nf, tm):
    t, d2 = nx.shape
    d = h.shape[1]
    w = y.shape[1]
    lg = wg.shape[1]
    return pl.pallas_call(
        _merge_kernel,
        grid=(t // tm,),
        in_specs=[_row_spec(tm, d2), _row_spec(tm, w), _row_spec(tm, w), _row_spec(tm, d),
                  _const_spec((d2, lg)), _const_spec((lg, w)),
                  _const_spec((d2, d)), _const_spec((d2, d)),
                  _const_spec((w, d)), _const_spec((w, d)), _const_spec((d, d)),
                  _const_spec((1, d))],
        out_specs=[_row_spec(tm, d), _row_spec(tm, d)],
        out_shape=[jax.ShapeDtypeStruct((t, d), F32), jax.ShapeDtypeStruct((t, d), BF16)],
        compiler_params=_cparams("parallel"),
        name="merge",
    )(nx, y, yb, h, wg, g2, wga, wgb, wa, wb, wo, nf)


def _bitonic_exchanges(n):
    out = []
    k = 2
    while k <= n:
        j = k // 2
        while j >= 1:
            for i in range(n):
                l = i ^ j
                if l > i:
                    out.append((i, l, (i & k) == 0))
            j //= 2
        k *= 2
    return out


def _top_values(s, count, rows):
    slabs = [s[i:i + SUBLANES] for i in range(0, s.shape[0], SUBLANES)]
    wires = 1
    while wires < len(slabs):
        wires *= 2
    slabs += [None] * (wires - len(slabs))
    for i, j, ascending in _bitonic_exchanges(wires):
        a, b = slabs[i], slabs[j]
        if a is None and b is None:
            continue
        if a is None or b is None:
            hi, lo = (a if b is None else b), None
        else:
            hi, lo = jnp.maximum(a, b), jnp.minimum(a, b)
        slabs[i], slabs[j] = (lo, hi) if ascending else (hi, lo)
    lists = slabs[::-1]
    neg = jnp.full((SUBLANES, s.shape[1]), -jnp.inf, s.dtype)
    lists = [neg if x is None else x for x in lists]
    tops = []
    for it in range(count):
        m = jnp.max(lists[0], axis=0, keepdims=True)
        tops.append(m)
        hit = lists[0] == m
        depth = min(count - it - 1, len(lists))
        for j in range(depth):
            below = lists[j + 1] if j + 1 < len(lists) else neg
            lists[j] = jnp.where(hit, below, lists[j])
    tops += [jnp.full_like(tops[0], -jnp.inf)] * (rows - count)
    return jnp.concatenate(tops, axis=0)


def _peer_select_kernel(n2_ref, h_ref, p_ref, wq_ref, keys_ref, wple_ref, wpg_ref,
                        thr_ref, s2_ref, e1_ref, e2_ref, hp_ref, q_scr):
    n2 = n2_ref[...]
    ple = _dot(p_ref[...], wple_ref[...]) * _sigmoid(_dot(n2, wpg_ref[...]))
    hp_ref[...] = h_ref[...] + ple
    q = _dot(n2, wq_ref[...]).astype(BF16)
    for hp in range(2 * PEER_HEADS):
        q_scr[hp] = q[:, hp * PEER_HALF:(hp + 1) * PEER_HALF]
    tt = n2.shape[0]
    kk = PEER_TOPK
    nt = (((1,), (1,)), ((), ()))

    def head(h, _):
        s1 = lax.dot_general(keys_ref[h, 0], q_scr[2 * h], nt, preferred_element_type=F32)
        s2 = lax.dot_general(keys_ref[h, 1], q_scr[2 * h + 1], nt, preferred_element_type=F32)
        def chunk_stats(c):
            cols = slice(c * LANES, (c + 1) * LANES)
            t1 = _top_values(s1[:, cols], kk + 1, 24)
            t2 = _top_values(s2[:, cols], kk + 1, 24)
            cand = [t1[0:1] + t2]
            cand += [t1[i:i + 1] + t2[0:8] for i in range(1, 8)]
            cand += [t1[8:24] + t2[0:1]]
            best = _top_values(jnp.concatenate(cand, axis=0), kk + 1, kk + 1)
            top = best[0:kk]
            inv_z = 1.0 / jnp.sum(jnp.exp(top - top[0:1]), axis=0, keepdims=True)
            tau = 0.5 * (best[kk - 1:kk] + best[kk:kk + 1])
            return tau, t1[0:1], t2[0:1], inv_z

        group = 2
        per = PEER_CHUNK // LANES
        for c0 in range(0, tt // LANES, group):
            stats = [chunk_stats(c) for c in range(c0, c0 + group)]
            for c, (tau, m1, m2, inv_z) in zip(range(c0, c0 + group), stats):
                cols = slice(c * LANES, (c + 1) * LANES)
                dst = (h, c // per, slice(None), slice((c % per) * LANES, (c % per + 1) * LANES))
                s2_ref[dst] = s2[:, cols]
                thr_ref[dst] = tau - s1[:, cols]
                e1_ref[dst] = jnp.exp(s1[:, cols] - m1)
                e2_ref[dst] = jnp.exp(s2[:, cols] - m2) * (0.5 * inv_z)
        return 0

    lax.fori_loop(0, PEER_HEADS, head, 0)


def _peer_select_call(n2, h, p, wq, keys, wple, wpg, tt):
    t, d = n2.shape
    pd = p.shape[1]
    dq = wq.shape[1]
    nh, nk = PEER_HEADS, PEER_KEYS
    tspec = pl.BlockSpec((nh, tt // PEER_CHUNK, nk, PEER_CHUNK), lambda i: (0, i, 0, 0))
    tshape = jax.ShapeDtypeStruct((nh, t // PEER_CHUNK, nk, PEER_CHUNK), F32)
    return pl.pallas_call(
        _peer_select_kernel,
        grid=(t // tt,),
        in_specs=[_row_spec(tt, d), _row_spec(tt, d), _row_spec(tt, pd),
                  _const_spec((d, dq)), _const_spec(keys.shape),
                  _const_spec((pd, d)), _const_spec((d, d))],
        out_specs=[tspec] * 4 + [_row_spec(tt, d)],
        out_shape=[tshape] * 4 + [jax.ShapeDtypeStruct((t, d), F32)],
        scratch_shapes=[pltpu.VMEM((2 * nh, tt, PEER_HALF), BF16)],
        compiler_params=_cparams("parallel"),
        name="peer_select",
    )(n2, h, p, wq, keys, wple, wpg)


def _peer_dense_kernel(n2_ref, u_ref, vt_ref, thr_ref, e1_ref, s2_ref, e2_ref, hp_ref,
                       out_ref, acc_ref, act0_ref, act1_ref, ga0_ref, ga1_ref, *, nj):
    f = pl.program_id(0)
    nt = (((1,), (1,)), ((), ()))
    nc, te, cw = act0_ref.shape
    nk = PEER_KEYS
    rb = 2 * SUBLANES
    j3 = jnp.maximum(f - 2, 0) % nj

    @pl.when(f == 0)
    def _():
        act1_ref[...] = jnp.zeros_like(act1_ref)
        ga0_ref[...] = jnp.zeros_like(ga0_ref)

    @pl.when(j3 == 0)
    def _():
        acc_ref[...] = jnp.zeros_like(acc_ref)

    def stages(act_new, act_cur, ga_cur, ga_old):
        for c in range(nc):
            tok = slice(c * cw, (c + 1) * cw)
            act_new[c] = lax.dot_general(u_ref[...], n2_ref[tok, :], nt,
                                         preferred_element_type=F32)
            for i1 in range(te // nk):
                thr = [thr_ref[h, c, i1:i1 + 1, :] for h in range(PEER_HEADS)]
                e1 = [e1_ref[h, c, i1:i1 + 1, :] for h in range(PEER_HEADS)]
                for r in range(nk // rb):
                    rs = slice(r * rb, (r + 1) * rb)
                    g = None
                    for h in range(PEER_HEADS):
                        keep = s2_ref[h, c, rs, :] >= thr[h]
                        term = jnp.where(keep, e2_ref[h, c, rs, :], 0.0) * e1[h]
                        g = term if g is None else g + term
                    rows = slice(i1 * nk + r * rb, i1 * nk + (r + 1) * rb)
                    ga_cur[c, rows, :] = (g * _gelu_x2(act_cur[c, rows, :])).astype(BF16)
            acc_ref[c] += _dot(vt_ref[...], ga_old[c])

    @pl.when(f % 2 == 0)
    def _():
        stages(act0_ref, act1_ref, ga1_ref, ga0_ref)

    @pl.when(f % 2 == 1)
    def _():
        stages(act1_ref, act0_ref, ga0_ref, ga1_ref)

    @pl.when(jnp.logical_and(f >= 2, j3 == nj - 1))
    def _():
        for c in range(nc):
            out_ref[c * cw:(c + 1) * cw, :] = hp_ref[c * cw:(c + 1) * cw, :] + acc_ref[c].T


def _peer_dense_call(n2, u, vt, thr, e1, s2, e2, hp, tt, te):
    t, d = n2.shape
    ne = u.shape[0]
    nh, nk = PEER_HEADS, PEER_KEYS
    r1 = te // nk
    cw = PEER_CHUNK
    nc = tt // cw
    ni, nj = t // tt, ne // te
    last = ni * nj - 1

    def pair(f, lag):
        a = jnp.clip(f - lag, 0, last)
        return a // nj, a % nj

    score_blk = (nh, nc, nk, cw)
    row_blk = (nh, nc, r1, cw)
    return pl.pallas_call(
        functools.partial(_peer_dense_kernel, nj=nj),
        grid=(ni * nj + 2,),
        in_specs=[pl.BlockSpec((tt, d), lambda f: (pair(f, 0)[0], 0)),
                  pl.BlockSpec((te, d), lambda f: (pair(f, 0)[1], 0)),
                  pl.BlockSpec((d, te), lambda f: (0, pair(f, 2)[1])),
                  pl.BlockSpec(row_blk, lambda f: (0, pair(f, 1)[0], pair(f, 1)[1], 0)),
                  pl.BlockSpec(row_blk, lambda f: (0, pair(f, 1)[0], pair(f, 1)[1], 0)),
                  pl.BlockSpec(score_blk, lambda f: (0, pair(f, 1)[0], 0, 0)),
                  pl.BlockSpec(score_blk, lambda f: (0, pair(f, 1)[0], 0, 0)),
                  pl.BlockSpec((tt, d), lambda f: (pair(f, 2)[0], 0))],
        out_specs=pl.BlockSpec((tt, d), lambda f: (pair(f, 2)[0], 0)),
        out_shape=jax.ShapeDtypeStruct((t, d), F32),
        scratch_shapes=[pltpu.VMEM((nc, d, cw), F32)]
        + [pltpu.VMEM((nc, te, cw), F32)] * 2 + [pltpu.VMEM((nc, te, cw), BF16)] * 2,
        compiler_params=_cparams("arbitrary"),
        name="peer_dense",
    )(n2, u, vt, thr, e1, s2, e2, hp)


def _final_norm_kernel(h_ref, g_ref, o_ref):
    o_ref[0] = _rms(h_ref[...], g_ref[...])


def _final_norm_call(h2d, g, batch, seq, ts):
    d = g.shape[-1]
    return pl.pallas_call(
        _final_norm_kernel,
        grid=(seq // ts, batch),
        in_specs=[pl.BlockSpec((ts, d), lambda s, b: (s, b)), _const_spec((1, d))],
        out_specs=pl.BlockSpec((1, ts, d), lambda s, b: (b, s, 0)),
        out_shape=jax.ShapeDtypeStruct((batch, seq, d), F32),
        compiler_params=_cparams("parallel", "parallel"),
        name="final_norm",
    )(h2d, g.reshape(1, d))


def _tile(total, want):
    want = min(want, total)
    assert total % want == 0, (total, want)
    return want


def kernel(x, p, norm_mix, norm_ffn, norm_final, w_in, rw_w0, rw_w2, rw_a0, rw_a2, rw_g2, rw_kk, rw_ka, rw_rk, rw_ln_g, rw_ln_b, rw_mv_w1, rw_mv_w2, rw_mv_v0, gm_ln_g, gm_ln_b, gm_ws, gm_bs, w_proj_a, w_proj_b, w_out, peer_wq, peer_keys, peer_u, peer_v, ple_w, ple_gate):
    batch, seq, d = x.shape
    depth = w_in.shape[0]
    t = batch * seq
    w = RW_HEADS * RW_HEAD_DIM
    assert batch * RW_HEADS == LANES and batch == SUBLANES and d == w
    assert seq % GM_CHUNK == 0

    tm = _tile(t, 512)
    tt = _tile(t, 512)
    te = 1024
    steps = _tile(seq, 64)

    gw = gm_ln_g.shape[1]
    sizes = (w, w, w, rw_w2.shape[1], rw_a2.shape[1], rw_g2.shape[1], gw, gw, d, d)
    offs = [0]
    for sz in sizes:
        offs.append(offs[-1] + sz)

    def col(wi, idx):
        return wi[:, offs[idx]:offs[idx + 1]].astype(BF16)

    h = x.transpose(1, 0, 2).reshape(t, d)
    pt = p.transpose(0, 2, 1, 3).reshape(depth, t, p.shape[-1]).astype(BF16)
    row = lambda a: a.reshape(1, -1).astype(F32)
    expand = jnp.repeat(jnp.eye(GM_CHUNK, dtype=F32), batch, axis=0)
    same_batch = jnp.tile(jnp.eye(batch, dtype=F32), (GM_CHUNK, GM_CHUNK))
    tril = jnp.tril(jnp.ones((GM_CHUNK, GM_CHUNK), F32))

    v_first = None
    for i in range(depth):
        wi = w_in[i]
        nx = _norm_shift(h, norm_mix[i], batch, tm)

        r, k, kkraw, ar = _rk_call(nx, col(wi, 0), col(wi, 1), col(wi, 4), rw_a2[i].astype(BF16),
                                   row(rw_a0[i]), row(rw_kk[i]), row(rw_ka[i]), tm)
        mv = None
        if i > 0:
            mv = (rw_mv_w1[i - 1].astype(BF16), rw_mv_w2[i - 1].astype(BF16), row(rw_mv_v0[i - 1]))
        v, dec = _vw_call(nx, col(wi, 2), col(wi, 3), rw_w2[i].astype(BF16), row(rw_w0[i]), mv,
                          v_first, tm)
        if i == 0:
            v_first = v

        y = _wkv_call(r, dec, k, v, kkraw, ar,
                      _head_param_lanes(rw_rk[i], batch),
                      _head_param_lanes(rw_ln_g[i].reshape(RW_HEADS, RW_HEAD_DIM), batch),
                      _head_param_lanes(rw_ln_b[i].reshape(RW_HEADS, RW_HEAD_DIM), batch),
                      batch, steps)

        ws_k = (jnp.einsum("ri,gij,cj->grc", expand, gm_ws[i] * tril, expand)
                * same_batch).astype(BF16)
        bias = jnp.repeat(gm_bs[i].T, batch, axis=0).astype(F32)
        yb = _gmlp_call(nx, col(wi, 6), col(wi, 7), row(gm_ln_g[i]), row(gm_ln_b[i]), ws_k, bias)

        h, n2 = _merge_call(nx, y, yb, h, col(wi, 5), rw_g2[i].astype(BF16), col(wi, 8), col(wi, 9),
                            w_proj_a[i].astype(BF16), w_proj_b[i].astype(BF16),
                            w_out[i].astype(BF16), row(norm_ffn[i]), tm)

        thr, s2, e1, e2, hp = _peer_select_call(
            n2, h, pt[i], peer_wq[i].astype(BF16), peer_keys[i].astype(BF16),
            ple_w[i].astype(BF16), ple_gate[i].astype(BF16), tt)
        h = _peer_dense_call(n2, peer_u[i].astype(BF16), peer_v[i].T.astype(BF16),
                             thr, e1, s2, e2, hp, tt, te)

    return _final_norm_call(h.reshape(seq, batch * d), norm_final, batch, seq, _tile(seq, 512))
```

```python
import functools
import math

import jax
import jax.numpy as jnp
from jax import lax
from jax.experimental import pallas as pl
from jax.experimental.pallas import tpu as pltpu

F32 = jnp.float32
BF16 = jnp.bfloat16

RW_HEADS = 16
RW_HEAD_DIM = 64
RW_LN_EPS = 64e-5
GM_GROUPS = 8
GM_CHUNK = 128
GM_LN_EPS = 1e-5
PEER_HEADS = 8
PEER_KEYS = 128
PEER_HALF = 128
PEER_TOPK = 16
PEER_CHUNK = 256
RMS_EPS = 1e-6

LANES = 128
SUBLANES = 8
VMEM_LIMIT = 56 * 1024 * 1024


def _cparams(*sem):
    return pltpu.CompilerParams(dimension_semantics=tuple(sem), vmem_limit_bytes=VMEM_LIMIT)


def _const_spec(shape):
    nd = len(shape)
    return pl.BlockSpec(shape, lambda *_: (0,) * nd, pipeline_mode=pl.Buffered(1))


def _row_spec(rows, cols):
    return pl.BlockSpec((rows, cols), lambda i: (i, 0))


def _gelu(x):
    c = math.sqrt(2.0 / math.pi)
    return 0.5 * x * (1.0 + jnp.tanh(c * (x + 0.044715 * (x * x * x))))


def _gelu_x2(x):
    c = math.sqrt(2.0 / math.pi)
    return x * (1.0 + jnp.tanh(x * (c + (0.044715 * c) * (x * x))))


def _sigmoid(x):
    return 1.0 / (1.0 + jnp.exp(-x))


def _dot(a, b):
    return jnp.dot(a, b, preferred_element_type=F32)


def _rms(x, g):
    return x * lax.rsqrt(jnp.mean(x * x, axis=-1, keepdims=True) + RMS_EPS) * g


def _norm_shift_kernel(h_ref, hprev_ref, g_ref, nx_ref, *, batch):
    i = pl.program_id(0)
    g = g_ref[...]
    n = _rms(h_ref[...], g)
    prev = _rms(hprev_ref[...], g) * (i > 0).astype(F32)
    d = n.shape[1]
    shifted = jnp.concatenate([prev, n[:-batch]], axis=0)
    nx_ref[:, :d] = n.astype(BF16)
    nx_ref[:, d:] = shifted.astype(BF16)


def _norm_shift(h, g, batch, tm):
    t, d = h.shape
    per = tm // batch
    return pl.pallas_call(
        functools.partial(_norm_shift_kernel, batch=batch),
        grid=(t // tm,),
        in_specs=[_row_spec(tm, d),
                  pl.BlockSpec((batch, d), lambda i: (jnp.maximum(i * per - 1, 0), 0)),
                  _const_spec((1, d))],
        out_specs=_row_spec(tm, 2 * d),
        out_shape=jax.ShapeDtypeStruct((t, 2 * d), BF16),
        compiler_params=_cparams("parallel"),
        name="norm_shift",
    )(h, h, g.reshape(1, d))


def _rk_kernel(nx_ref, wr_ref, wk_ref, wa_ref, a2_ref, a0_ref, kk_ref, ka_ref,
               r_ref, k_ref, kkraw_ref, ar_ref):
    nx = nx_ref[...]
    r_ref[...] = _dot(nx, wr_ref[...])
    zk = _dot(nx, wk_ref[...])
    za = _dot(nx, wa_ref[...])
    a_rate = _sigmoid(a0_ref[...] + _dot(za.astype(BF16), a2_ref[...]))
    ar_ref[...] = a_rate
    kkraw_ref[...] = zk * kk_ref[...]
    k_ref[...] = zk * (1.0 + (a_rate - 1.0) * ka_ref[...])


def _rk_call(nx, wr, wk, wa, a2, a0, kk, ka, tm):
    t, d2 = nx.shape
    w = wr.shape[1]
    la = wa.shape[1]
    out = jax.ShapeDtypeStruct((t, w), F32)
    return pl.pallas_call(
        _rk_kernel,
        grid=(t // tm,),
        in_specs=[_row_spec(tm, d2), _const_spec((d2, w)), _const_spec((d2, w)),
                  _const_spec((d2, la)), _const_spec((la, w)),
                  _const_spec((1, w)), _const_spec((1, w)), _const_spec((1, w))],
        out_specs=[_row_spec(tm, w)] * 4,
        out_shape=[out] * 4,
        compiler_params=_cparams("parallel"),
        name="rwkv_rk",
    )(nx, wr, wk, wa, a2, a0, kk, ka)


def _decay_from(nx, ww_ref, w2_ref, w0_ref):
    zw = _dot(nx, ww_ref[...])
    x = -(w0_ref[...] + _dot(jnp.tanh(zw).astype(BF16), w2_ref[...]))
    softplus = jnp.maximum(x, 0.0) + jnp.log(1.0 + jnp.exp(-jnp.abs(x)))
    return jnp.exp(-jnp.exp(-softplus - 0.5))


def _vw0_kernel(nx_ref, wv_ref, ww_ref, w2_ref, w0_ref, v_ref, dec_ref):
    nx = nx_ref[...]
    v_ref[...] = _dot(nx, wv_ref[...])
    dec_ref[...] = _decay_from(nx, ww_ref, w2_ref, w0_ref)


def _vw_kernel(nx_ref, wv_ref, ww_ref, w2_ref, w0_ref, m1_ref, m2_ref, v0_ref, vfirst_ref,
               v_ref, dec_ref):
    nx = nx_ref[...]
    zv = _dot(nx, wv_ref[...])
    lo = _dot(nx, m1_ref[...])
    vmix = _sigmoid(v0_ref[...] + _dot(lo.astype(BF16), m2_ref[...]))
    v_ref[...] = zv + (vfirst_ref[...] - zv) * vmix
    dec_ref[...] = _decay_from(nx, ww_ref, w2_ref, w0_ref)


def _vw_call(nx, wv, ww, w2, w0, mv, vfirst, tm):
    t, d2 = nx.shape
    w = wv.shape[1]
    lw = ww.shape[1]
    out = jax.ShapeDtypeStruct((t, w), F32)
    in_specs = [_row_spec(tm, d2), _const_spec((d2, w)), _const_spec((d2, lw)),
                _const_spec((lw, w)), _const_spec((1, w))]
    args = [nx, wv, ww, w2, w0]
    if mv is None:
        body = _vw0_kernel
    else:
        m1, m2, v0 = mv
        lm = m1.shape[1]
        body = _vw_kernel
        in_specs += [_const_spec((d2, lm)), _const_spec((lm, w)), _const_spec((1, w)),
                     _row_spec(tm, w)]
        args += [m1, m2, v0, vfirst]
    return pl.pallas_call(
        body,
        grid=(t // tm,),
        in_specs=in_specs,
        out_specs=[_row_spec(tm, w)] * 2,
        out_shape=[out] * 2,
        compiler_params=_cparams("parallel"),
        name="rwkv_vw",
    )(*args)


def _to_lanes(blk):
    r = jnp.concatenate([blk[s * 8:(s + 1) * 8, j * LANES:(j + 1) * LANES]
                         for s in range(2) for j in range(8)], axis=0)
    tr = r.T
    lo, hi = tr[0:64], tr[64:128]
    lane = lax.broadcasted_iota(jnp.int32, (64, LANES), 1)
    first = lane < 64
    return (jnp.where(first, lo, pltpu.roll(hi, 64, 1)),
            jnp.where(first, pltpu.roll(lo, 64, 1), hi))


def _from_lanes(o0, o1):
    lane = lax.broadcasted_iota(jnp.int32, (64, LANES), 1)
    first = lane < 64
    lo = jnp.where(first, o0, pltpu.roll(o1, 64, 1))
    hi = jnp.where(first, pltpu.roll(o0, 64, 1), o1)
    r = jnp.concatenate([lo, hi], axis=0).T
    rows = [jnp.concatenate([r[s * 64 + j * 8:s * 64 + j * 8 + 8] for j in range(8)], axis=1)
            for s in range(2)]
    return jnp.concatenate(rows, axis=0)


_WR, _WW, _WK, _WV, _WA, _WB = range(6)


def _wkv_kernel(r_ref, w_ref, k_ref, v_ref, kk_ref, ar_ref, rk_ref, lng_ref, lnb_ref,
                y_ref, st_ref, buf_a, buf_b, y_a, y_b, *, steps):
    n = RW_HEAD_DIM
    pairs = steps // 2

    @pl.when(pl.program_id(0) == 0)
    def _():
        st_ref[...] = jnp.zeros_like(st_ref)

    def load_pair(sp, buf):
        rows = pl.ds(pl.multiple_of(sp * 16, 16), 16)
        for idx, src in ((_WR, r_ref), (_WW, w_ref), (_WK, k_ref), (_WV, v_ref)):
            o0, o1 = _to_lanes(src[rows, :])
            buf[idx, 0] = o0
            buf[idx, 1] = o1
        kk0, kk1 = _to_lanes(kk_ref[rows, :])
        ar0, ar1 = _to_lanes(ar_ref[rows, :])
        for off, kk, ar in ((0, kk0, ar0), (1, kk1, ar1)):
            ss = jnp.sum(kk * kk, axis=0, keepdims=True)
            kkn = kk * lax.rsqrt(jnp.maximum(ss, 1e-24))
            buf[_WA, off] = -kkn
            buf[_WB, off] = kkn * ar

    def step(buf, off, nbuf, noff, ybuf, sa):
        nxt = []
        hv = n // 4
        for half in range(n // hv):
            vs = slice(half * hv, (half + 1) * hv)
            sa_h = sa[vs]
            vv = buf[_WV, off, vs, :]
            y = jnp.zeros((hv, LANES), F32)
            san = jnp.zeros((hv, LANES), F32)
            for k in range(n):
                row = slice(k, k + 1)
                sk = (st_ref[k, vs, :] * buf[_WW, off, row, :] + sa_h * buf[_WB, off, row, :]
                      + vv * buf[_WK, off, row, :])
                st_ref[k, vs, :] = sk
                y = y + sk * buf[_WR, off, row, :]
                san = san + sk * nbuf[_WA, noff, row, :]
            ybuf[off, vs, :] = y
            nxt.append(san)
        return jnp.concatenate(nxt, axis=0)

    rk = rk_ref[...]
    lng = lng_ref[...]
    lnb = lnb_ref[...]

    def finish(buf, off, ybuf):
        y = ybuf[off]
        mu = jnp.mean(y, axis=0, keepdims=True)
        yc = y - mu
        var = jnp.mean(yc * yc, axis=0, keepdims=True)
        yn = yc * lax.rsqrt(var + RW_LN_EPS) * lng + lnb
        bonus = jnp.sum(buf[_WR, off] * buf[_WK, off] * rk, axis=0, keepdims=True)
        return yn + bonus * buf[_WV, off]

    def store_pair(sp, buf, ybuf):
        rows = pl.ds(pl.multiple_of(sp * 16, 16), 16)
        y_ref[rows, :] = _from_lanes(finish(buf, 0, ybuf), finish(buf, 1, ybuf))

    load_pair(0, buf_a)
    sa0 = jnp.zeros((n, LANES), F32)
    for k in range(n):
        sa0 = sa0 + st_ref[k] * buf_a[_WA, 0, k:k + 1, :]

    def two_pairs(q, sa):
        sa = step(buf_a, 0, buf_a, 1, y_a, sa)
        load_pair(2 * q + 1, buf_b)
        sa = step(buf_a, 1, buf_b, 0, y_a, sa)
        sa = step(buf_b, 0, buf_b, 1, y_b, sa)
        store_pair(2 * q, buf_a, y_a)
        load_pair(jnp.minimum(2 * q + 2, pairs - 1), buf_a)
        sa = step(buf_b, 1, buf_a, 0, y_b, sa)
        store_pair(2 * q + 1, buf_b, y_b)
        return sa

    lax.fori_loop(0, pairs // 2, two_pairs, sa0)


def _wkv_call(r, w, k, v, kkraw, ar, rk_l, lng_l, lnb_l, batch, steps):
    t, width = r.shape
    rows = steps * batch
    n = RW_HEAD_DIM
    assert steps % 4 == 0
    pair_buf = pltpu.VMEM((6, 2, n, LANES), F32)
    pair_out = pltpu.VMEM((2, n, LANES), F32)
    return pl.pallas_call(
        functools.partial(_wkv_kernel, steps=steps),
        grid=(t // rows,),
        in_specs=[_row_spec(rows, width)] * 6 + [_const_spec((n, LANES))] * 3,
        out_specs=_row_spec(rows, width),
        out_shape=jax.ShapeDtypeStruct((t, width), F32),
        scratch_shapes=[pltpu.VMEM((n, n, LANES), F32), pair_buf, pair_buf, pair_out, pair_out],
        compiler_params=_cparams("arbitrary"),
        name="wkv7_scan",
    )(r, w, k, v, kkraw, ar, rk_l, lng_l, lnb_l)


def _head_param_lanes(pv, batch):
    a = pv.reshape(RW_HEADS // 2, 2, RW_HEAD_DIM).transpose(2, 1, 0)
    a = jnp.broadcast_to(a[..., None], a.shape + (batch,))
    return a.reshape(RW_HEAD_DIM, LANES).astype(F32)


def _gmlp_kernel(nx_ref, wu_ref, wg_ref, lng_ref, lnb_ref, ws_ref, bias_ref, yb_ref):
    nx = nx_ref[...]
    u = _gelu(_dot(nx, wu_ref[...]))
    gv = _gelu(_dot(nx, wg_ref[...]))
    mu = jnp.mean(gv, axis=-1, keepdims=True)
    gc = gv - mu
    var = jnp.mean(gc * gc, axis=-1, keepdims=True)
    v = (gc * lax.rsqrt(var + GM_LN_EPS) * lng_ref[...] + lnb_ref[...]).astype(BF16)
    gd = v.shape[1] // GM_GROUPS
    for g in range(GM_GROUPS):
        cols = slice(g * gd, (g + 1) * gd)
        s = _dot(ws_ref[g], v[:, cols]) + bias_ref[:, g:g + 1]
        yb_ref[:, cols] = (u[:, cols] * s).astype(BF16)


def _gmlp_call(nx, wu, wg, lng, lnb, ws_k, bias):
    t, d2 = nx.shape
    w = wu.shape[1]
    rows = ws_k.shape[1]
    return pl.pallas_call(
        _gmlp_kernel,
        grid=(t // rows,),
        in_specs=[_row_spec(rows, d2), _const_spec((d2, w)), _const_spec((d2, w)),
                  _const_spec((1, w)), _const_spec((1, w)),
                  _const_spec(ws_k.shape), _const_spec((rows, GM_GROUPS))],
        out_specs=_row_spec(rows, w),
        out_shape=jax.ShapeDtypeStruct((t, w), BF16),
        compiler_params=_cparams("parallel"),
        name="gmlp",
    )(nx, wu, wg, lng, lnb, ws_k, bias)


def _merge_kernel(nx_ref, y_ref, yb_ref, h_ref, wg_ref, g2_ref, wga_ref, wgb_ref,
                  wa_ref, wb_ref, wo_ref, nf_ref, hn_ref, n2_ref):
    nx = nx_ref[...]
    gate = _dot(_sigmoid(_dot(nx, wg_ref[...])).astype(BF16), g2_ref[...])
    ya = (y_ref[...] * gate).astype(BF16)
    pa = _dot(ya, wa_ref[...])
    pb = _dot(yb_ref[...], wb_ref[...])
    merged = _sigmoid(_dot(nx, wga_ref[...])) * pa + _sigmoid(_dot(nx, wgb_ref[...])) * pb
    hn = h_ref[...] + _dot(merged.astype(BF16), wo_ref[...])
    hn_ref[...] = hn
    n2_ref[...] = _rms(hn, nf_ref[...]).astype(BF16)


def _merge_call(nx, y, yb, h, wg, g2, wga, wgb, wa, wb, wo, nf, tm):
    t, d2 = nx.shape
    d = h.shape[1]
    w = y.shape[1]
    lg = wg.shape[1]
    return pl.pallas_call(
        _merge_kernel,
        grid=(t // tm,),
        in_specs=[_row_spec(tm, d2), _row_spec(tm, w), _row_spec(tm, w), _row_spec(tm, d),
                  _const_spec((d2, lg)), _const_spec((lg, w)),
                  _const_spec((d2, d)), _const_spec((d2, d)),
                  _const_spec((w, d)), _const_spec((w, d)), _const_spec((d, d)),
                  _const_spec((1, d))],
        out_specs=[_row_spec(tm, d), _row_spec(tm, d)],
        out_shape=[jax.ShapeDtypeStruct((t, d), F32), jax.ShapeDtypeStruct((t, d), BF16)],
        compiler_params=_cparams("parallel"),
        name="merge",
    )(nx, y, yb, h, wg, g2, wga, wgb, wa, wb, wo, nf)


def _bitonic_exchanges(n):
    out = []
    k = 2
    while k <= n:
        j = k // 2
        while j >= 1:
            for i in range(n):
                l = i ^ j
                if l > i:
                    out.append((i, l, (i & k) == 0))
            j //= 2
        k *= 2
    return out


def _top_values(s, count, rows):
    slabs = [s[i:i + SUBLANES] for i in range(0, s.shape[0], SUBLANES)]
    wires = 1
    while wires < len(slabs):
        wires *= 2
    slabs += [None] * (wires - len(slabs))
    for i, j, ascending in _bitonic_exchanges(wires):
        a, b = slabs[i], slabs[j]
        if a is None and b is None:
            continue
        if a is None or b is None:
            hi, lo = (a if b is None else b), None
        else:
            hi, lo = jnp.maximum(a, b), jnp.minimum(a, b)
        slabs[i], slabs[j] = (lo, hi) if ascending else (hi, lo)
    lists = slabs[::-1]
    neg = jnp.full((SUBLANES, s.shape[1]), -jnp.inf, s.dtype)
    lists = [neg if x is None else x for x in lists]
    tops = []
    for it in range(count):
        m = jnp.max(lists[0], axis=0, keepdims=True)
        tops.append(m)
        hit = lists[0] == m
        depth = min(count - it - 1, len(lists))
        for j in range(depth):
            below = lists[j + 1] if j + 1 < len(lists) else neg
            lists[j] = jnp.where(hit, below, lists[j])
    tops += [jnp.full_like(tops[0], -jnp.inf)] * (rows - count)
    return jnp.concatenate(tops, axis=0)


def _peer_select_kernel(n2_ref, h_ref, p_ref, wq_ref, keys_ref, wple_ref, wpg_ref,
                        thr_ref, s2_ref, e1_ref, e2_ref, hp_ref, q_scr):
    n2 = n2_ref[...]
    ple = _dot(p_ref[...], wple_ref[...]) * _sigmoid(_dot(n2, wpg_ref[...]))
    hp_ref[...] = h_ref[...] + ple
    q = _dot(n2, wq_ref[...]).astype(BF16)
    for hp in range(2 * PEER_HEADS):
        q_scr[hp] = q[:, hp * PEER_HALF:(hp + 1) * PEER_HALF]
    tt = n2.shape[0]
    kk = PEER_TOPK
    nt = (((1,), (1,)), ((), ()))

    def head(h, _):
        s1 = lax.dot_general(keys_ref[h, 0], q_scr[2 * h], nt, preferred_element_type=F32)
        s2 = lax.dot_general(keys_ref[h, 1], q_scr[2 * h + 1], nt, preferred_element_type=F32)
        def chunk_stats(c):
            cols = slice(c * LANES, (c + 1) * LANES)
            t1 = _top_values(s1[:, cols], kk + 1, 24)
            t2 = _top_values(s2[:, cols], kk + 1, 24)
            cand = [t1[0:1] + t2]
            cand += [t1[i:i + 1] + t2[0:8] for i in range(1, 8)]
            cand += [t1[8:24] + t2[0:1]]
            best = _top_values(jnp.concatenate(cand, axis=0), kk + 1, kk + 1)
            top = best[0:kk]
            inv_z = 1.0 / jnp.sum(jnp.exp(top - top[0:1]), axis=0, keepdims=True)
            tau = 0.5 * (best[kk - 1:kk] + best[kk:kk + 1])
            return tau, t1[0:1], t2[0:1], inv_z

        group = 2
        per = PEER_CHUNK // LANES
        for c0 in range(0, tt // LANES, group):
            stats = [chunk_stats(c) for c in range(c0, c0 + group)]
            for c, (tau, m1, m2, inv_z) in zip(range(c0, c0 + group), stats):
                cols = slice(c * LANES, (c + 1) * LANES)
                dst = (h, c // per, slice(None), slice((c % per) * LANES, (c % per + 1) * LANES))
                s2_ref[dst] = s2[:, cols]
                thr_ref[dst] = tau - s1[:, cols]
                e1_ref[dst] = jnp.exp(s1[:, cols] - m1)
                e2_ref[dst] = jnp.exp(s2[:, cols] - m2) * (0.5 * inv_z)
        return 0

    lax.fori_loop(0, PEER_HEADS, head, 0)


def _peer_select_call(n2, h, p, wq, keys, wple, wpg, tt):
    t, d = n2.shape
    pd = p.shape[1]
    dq = wq.shape[1]
    nh, nk = PEER_HEADS, PEER_KEYS
    tspec = pl.BlockSpec((nh, tt // PEER_CHUNK, nk, PEER_CHUNK), lambda i: (0, i, 0, 0))
    tshape = jax.ShapeDtypeStruct((nh, t // PEER_CHUNK, nk, PEER_CHUNK), F32)
    return pl.pallas_call(
        _peer_select_kernel,
        grid=(t // tt,),
        in_specs=[_row_spec(tt, d), _row_spec(tt, d), _row_spec(tt, pd),
                  _const_spec((d, dq)), _const_spec(keys.shape),
                  _const_spec((pd, d)), _const_spec((d, d))],
        out_specs=[tspec] * 4 + [_row_spec(tt, d)],
        out_shape=[tshape] * 4 + [jax.ShapeDtypeStruct((t, d), F32)],
        scratch_shapes=[pltpu.VMEM((2 * nh, tt, PEER_HALF), BF16)],
        compiler_params=_cparams("parallel"),
        name="peer_select",
    )(n2, h, p, wq, keys, wple, wpg)


def _peer_dense_kernel(n2_ref, u_ref, vt_ref, thr_ref, e1_ref, s2_ref, e2_ref, hp_ref,
                       out_ref, acc_ref, act0_ref, act1_ref, ga0_ref, ga1_ref, *, nj):
    f = pl.program_id(0)
    nt = (((1,), (1,)), ((), ()))
    nc, te, cw = act0_ref.shape
    nk = PEER_KEYS
    rb = 2 * SUBLANES
    j3 = jnp.maximum(f - 2, 0) % nj

    @pl.when(f == 0)
    def _():
        act1_ref[...] = jnp.zeros_like(act1_ref)
        ga0_ref[...] = jnp.zeros_like(ga0_ref)

    @pl.when(j3 == 0)
    def _():
        acc_ref[...] = jnp.zeros_like(acc_ref)

    def stages(act_new, act_cur, ga_cur, ga_old):
        for c in range(nc):
            tok = slice(c * cw, (c + 1) * cw)
            act_new[c] = lax.dot_general(u_ref[...], n2_ref[tok, :], nt,
                                         preferred_element_type=F32)
            for i1 in range(te // nk):
                thr = [thr_ref[h, c, i1:i1 + 1, :] for h in range(PEER_HEADS)]
                e1 = [e1_ref[h, c, i1:i1 + 1, :] for h in range(PEER_HEADS)]
                for r in range(nk // rb):
                    rs = slice(r * rb, (r + 1) * rb)
                    g = None
                    for h in range(PEER_HEADS):
                        keep = s2_ref[h, c, rs, :] >= thr[h]
                        term = jnp.where(keep, e2_ref[h, c, rs, :], 0.0) * e1[h]
                        g = term if g is None else g + term
                    rows = slice(i1 * nk + r * rb, i1 * nk + (r + 1) * rb)
                    ga_cur[c, rows, :] = (g.astype(BF16)
                                          * _gelu_x2(act_cur[c, rows, :].astype(BF16)))
            acc_ref[c] += _dot(vt_ref[...], ga_old[c])

    @pl.when(f % 2 == 0)
    def _():
        stages(act0_ref, act1_ref, ga1_ref, ga0_ref)

    @pl.when(f % 2 == 1)
    def _():
        stages(act1_ref, act0_ref, ga0_ref, ga1_ref)

    @pl.when(jnp.logical_and(f >= 2, j3 == nj - 1))
    def _():
        for c in range(nc):
            out_ref[c * cw:(c + 1) * cw, :] = hp_ref[c * cw:(c + 1) * cw, :] + acc_ref[c].T


def _peer_dense_call(n2, u, vt, thr, e1, s2, e2, hp, tt, te):
    t, d = n2.shape
    ne = u.shape[0]
    nh, nk = PEER_HEADS, PEER_KEYS
    r1 = te // nk
    cw = PEER_CHUNK
    nc = tt // cw
    ni, nj = t // tt, ne // te
    last = ni * nj - 1

    def pair(f, lag):
        a = jnp.clip(f - lag, 0, last)
        return a // nj, a % nj

    score_blk = (nh, nc, nk, cw)
    row_blk = (nh, nc, r1, cw)
    return pl.pallas_call(
        functools.partial(_peer_dense_kernel, nj=nj),
        grid=(ni * nj + 2,),
        in_specs=[pl.BlockSpec((tt, d), lambda f: (pair(f, 0)[0], 0)),
                  pl.BlockSpec((te, d), lambda f: (pair(f, 0)[1], 0)),
                  pl.BlockSpec((d, te), lambda f: (0, pair(f, 2)[1])),
                  pl.BlockSpec(row_blk, lambda f: (0, pair(f, 1)[0], pair(f, 1)[1], 0)),
                  pl.BlockSpec(row_blk, lambda f: (0, pair(f, 1)[0], pair(f, 1)[1], 0)),
                  pl.BlockSpec(score_blk, lambda f: (0, pair(f, 1)[0], 0, 0)),
                  pl.BlockSpec(score_blk, lambda f: (0, pair(f, 1)[0], 0, 0)),
                  pl.BlockSpec((tt, d), lambda f: (pair(f, 2)[0], 0))],
        out_specs=pl.BlockSpec((tt, d), lambda f: (pair(f, 2)[0], 0)),
        out_shape=jax.ShapeDtypeStruct((t, d), F32),
        scratch_shapes=[pltpu.VMEM((nc, d, cw), F32)]
        + [pltpu.VMEM((nc, te, cw), F32)] * 2 + [pltpu.VMEM((nc, te, cw), BF16)] * 2,
        compiler_params=_cparams("arbitrary"),
        name="peer_dense",
    )(n2, u, vt, thr, e1, s2, e2, hp)


def _final_norm_kernel(h_ref, g_ref, o_ref):
    o_ref[0] = _rms(h_ref[...], g_ref[...])


def _final_norm_call(h2d, g, batch, seq, ts):
    d = g.shape[-1]
    return pl.pallas_call(
        _final_norm_kernel,
        grid=(seq // ts, batch),
        in_specs=[pl.BlockSpec((ts, d), lambda s, b: (s, b)), _const_spec((1, d))],
        out_specs=pl.BlockSpec((1, ts, d), lambda s, b: (b, s, 0)),
        out_shape=jax.ShapeDtypeStruct((batch, seq, d), F32),
        compiler_params=_cparams("parallel", "parallel"),
        name="final_norm",
    )(h2d, g.reshape(1, d))


def _tile(total, want):
    want = min(want, total)
    assert total % want == 0, (total, want)
    return want


def kernel(x, p, norm_mix, norm_ffn, norm_final, w_in, rw_w0, rw_w2, rw_a0, rw_a2, rw_g2, rw_kk, rw_ka, rw_rk, rw_ln_g, rw_ln_b, rw_mv_w1, rw_mv_w2, rw_mv_v0, gm_ln_g, gm_ln_b, gm_ws, gm_bs, w_proj_a, w_proj_b, w_out, peer_wq, peer_keys, peer_u, peer_v, ple_w, ple_gate):
    batch, seq, d = x.shape
    depth = w_in.shape[0]
    t = batch * seq
    w = RW_HEADS * RW_HEAD_DIM
    assert batch * RW_HEADS == LANES and batch == SUBLANES and d == w
    assert seq % GM_CHUNK == 0

    tm = _tile(t, 512)
    tt = _tile(t, 512)
    te = 1024
    steps = _tile(seq, 64)

    gw = gm_ln_g.shape[1]
    sizes = (w, w, w, rw_w2.shape[1], rw_a2.shape[1], rw_g2.shape[1], gw, gw, d, d)
    offs = [0]
    for sz in sizes:
        offs.append(offs[-1] + sz)

    def col(wi, idx):
        return wi[:, offs[idx]:offs[idx + 1]].astype(BF16)

    h = x.transpose(1, 0, 2).reshape(t, d)
    pt = p.transpose(0, 2, 1, 3).reshape(depth, t, p.shape[-1]).astype(BF16)
    row = lambda a: a.reshape(1, -1).astype(F32)
    expand = jnp.repeat(jnp.eye(GM_CHUNK, dtype=F32), batch, axis=0)
    same_batch = jnp.tile(jnp.eye(batch, dtype=F32), (GM_CHUNK, GM_CHUNK))
    tril = jnp.tril(jnp.ones((GM_CHUNK, GM_CHUNK), F32))

    v_first = None
    for i in range(depth):
        wi = w_in[i]
        nx = _norm_shift(h, norm_mix[i], batch, tm)

        r, k, kkraw, ar = _rk_call(nx, col(wi, 0), col(wi, 1), col(wi, 4), rw_a2[i].astype(BF16),
                                   row(rw_a0[i]), row(rw_kk[i]), row(rw_ka[i]), tm)
        mv = None
        if i > 0:
            mv = (rw_mv_w1[i - 1].astype(BF16), rw_mv_w2[i - 1].astype(BF16), row(rw_mv_v0[i - 1]))
        v, dec = _vw_call(nx, col(wi, 2), col(wi, 3), rw_w2[i].astype(BF16), row(rw_w0[i]), mv,
                          v_first, tm)
        if i == 0:
            v_first = v

        y = _wkv_call(r, dec, k, v, kkraw, ar,
                      _head_param_lanes(rw_rk[i], batch),
                      _head_param_lanes(rw_ln_g[i].reshape(RW_HEADS, RW_HEAD_DIM), batch),
                      _head_param_lanes(rw_ln_b[i].reshape(RW_HEADS, RW_HEAD_DIM), batch),
                      batch, steps)

        ws_k = (jnp.einsum("ri,gij,cj->grc", expand, gm_ws[i] * tril, expand)
                * same_batch).astype(BF16)
        bias = jnp.repeat(gm_bs[i].T, batch, axis=0).astype(F32)
        yb = _gmlp_call(nx, col(wi, 6), col(wi, 7), row(gm_ln_g[i]), row(gm_ln_b[i]), ws_k, bias)

        h, n2 = _merge_call(nx, y, yb, h, col(wi, 5), rw_g2[i].astype(BF16), col(wi, 8), col(wi, 9),
                            w_proj_a[i].astype(BF16), w_proj_b[i].astype(BF16),
                            w_out[i].astype(BF16), row(norm_ffn[i]), tm)

        thr, s2, e1, e2, hp = _peer_select_call(
            n2, h, pt[i], peer_wq[i].astype(BF16), peer_keys[i].astype(BF16),
            ple_w[i].astype(BF16), ple_gate[i].astype(BF16), tt)
        h = _peer_dense_call(n2, peer_u[i].astype(BF16), peer_v[i].T.astype(BF16),
                             thr, e1, s2, e2, hp, tt, te)

    return _final_norm_call(h.reshape(seq, batch * d), norm_final, batch, seq, _tile(seq, 512))
```

```python
import functools
import math

import jax
import jax.numpy as jnp
from jax import lax
from jax.experimental import pallas as pl
from jax.experimental.pallas import tpu as pltpu

F32 = jnp.float32
BF16 = jnp.bfloat16

RW_HEADS = 16
RW_HEAD_DIM = 64
RW_LN_EPS = 64e-5
GM_GROUPS = 8
GM_CHUNK = 128
GM_LN_EPS = 1e-5
PEER_HEADS = 8
PEER_KEYS = 128
PEER_HALF = 128
PEER_TOPK = 16
PEER_CHUNK = 256
RMS_EPS = 1e-6

LANES = 128
SUBLANES = 8
VMEM_LIMIT = 56 * 1024 * 1024


def _cparams(*sem):
    return pltpu.CompilerParams(dimension_semantics=tuple(sem), vmem_limit_bytes=VMEM_LIMIT)


def _const_spec(shape):
    nd = len(shape)
    return pl.BlockSpec(shape, lambda *_: (0,) * nd, pipeline_mode=pl.Buffered(1))


def _row_spec(rows, cols):
    return pl.BlockSpec((rows, cols), lambda i: (i, 0))


def _gelu(x):
    c = math.sqrt(2.0 / math.pi)
    return 0.5 * x * (1.0 + jnp.tanh(c * (x + 0.044715 * (x * x * x))))


def _gelu_x2(x):
    c = math.sqrt(2.0 / math.pi)
    return x * (1.0 + jnp.tanh(x * (c + (0.044715 * c) * (x * x))))


def _sigmoid(x):
    return 1.0 / (1.0 + jnp.exp(-x))


def _dot(a, b):
    return jnp.dot(a, b, preferred_element_type=F32)


def _rms(x, g):
    return x * lax.rsqrt(jnp.mean(x * x, axis=-1, keepdims=True) + RMS_EPS) * g


def _norm_shift_kernel(h_ref, hprev_ref, g_ref, nx_ref, *, batch):
    i = pl.program_id(0)
    g = g_ref[...]
    n = _rms(h_ref[...], g)
    prev = _rms(hprev_ref[...], g) * (i > 0).astype(F32)
    d = n.shape[1]
    shifted = jnp.concatenate([prev, n[:-batch]], axis=0)
    nx_ref[:, :d] = n.astype(BF16)
    nx_ref[:, d:] = shifted.astype(BF16)


def _norm_shift(h, g, batch, tm):
    t, d = h.shape
    per = tm // batch
    return pl.pallas_call(
        functools.partial(_norm_shift_kernel, batch=batch),
        grid=(t // tm,),
        in_specs=[_row_spec(tm, d),
                  pl.BlockSpec((batch, d), lambda i: (jnp.maximum(i * per - 1, 0), 0)),
                  _const_spec((1, d))],
        out_specs=_row_spec(tm, 2 * d),
        out_shape=jax.ShapeDtypeStruct((t, 2 * d), BF16),
        compiler_params=_cparams("parallel"),
        name="norm_shift",
    )(h, h, g.reshape(1, d))


def _rk_kernel(nx_ref, wr_ref, wk_ref, wa_ref, a2_ref, a0_ref, kk_ref, ka_ref,
               r_ref, k_ref, kkraw_ref, ar_ref):
    nx = nx_ref[...]
    r_ref[...] = _dot(nx, wr_ref[...])
    zk = _dot(nx, wk_ref[...])
    za = _dot(nx, wa_ref[...])
    a_rate = _sigmoid(a0_ref[...] + _dot(za.astype(BF16), a2_ref[...]))
    ar_ref[...] = a_rate
    kkraw_ref[...] = zk * kk_ref[...]
    k_ref[...] = zk * (1.0 + (a_rate - 1.0) * ka_ref[...])


def _rk_call(nx, wr, wk, wa, a2, a0, kk, ka, tm):
    t, d2 = nx.shape
    w = wr.shape[1]
    la = wa.shape[1]
    out = jax.ShapeDtypeStruct((t, w), F32)
    return pl.pallas_call(
        _rk_kernel,
        grid=(t // tm,),
        in_specs=[_row_spec(tm, d2), _const_spec((d2, w)), _const_spec((d2, w)),
                  _const_spec((d2, la)), _const_spec((la, w)),
                  _const_spec((1, w)), _const_spec((1, w)), _const_spec((1, w))],
        out_specs=[_row_spec(tm, w)] * 4,
        out_shape=[out] * 4,
        compiler_params=_cparams("parallel"),
        name="rwkv_rk",
    )(nx, wr, wk, wa, a2, a0, kk, ka)


def _decay_from(nx, ww_ref, w2_ref, w0_ref):
    zw = _dot(nx, ww_ref[...])
    x = -(w0_ref[...] + _dot(jnp.tanh(zw).astype(BF16), w2_ref[...]))
    softplus = jnp.maximum(x, 0.0) + jnp.log(1.0 + jnp.exp(-jnp.abs(x)))
    return jnp.exp(-jnp.exp(-softplus - 0.5))


def _vw0_kernel(nx_ref, wv_ref, ww_ref, w2_ref, w0_ref, v_ref, dec_ref):
    nx = nx_ref[...]
    v_ref[...] = _dot(nx, wv_ref[...])
    dec_ref[...] = _decay_from(nx, ww_ref, w2_ref, w0_ref)


def _vw_kernel(nx_ref, wv_ref, ww_ref, w2_ref, w0_ref, m1_ref, m2_ref, v0_ref, vfirst_ref,
               v_ref, dec_ref):
    nx = nx_ref[...]
    zv = _dot(nx, wv_ref[...])
    lo = _dot(nx, m1_ref[...])
    vmix = _sigmoid(v0_ref[...] + _dot(lo.astype(BF16), m2_ref[...]))
    v_ref[...] = zv + (vfirst_ref[...] - zv) * vmix
    dec_ref[...] = _decay_from(nx, ww_ref, w2_ref, w0_ref)


def _vw_call(nx, wv, ww, w2, w0, mv, vfirst, tm):
    t, d2 = nx.shape
    w = wv.shape[1]
    lw = ww.shape[1]
    out = jax.ShapeDtypeStruct((t, w), F32)
    in_specs = [_row_spec(tm, d2), _const_spec((d2, w)), _const_spec((d2, lw)),
                _const_spec((lw, w)), _const_spec((1, w))]
    args = [nx, wv, ww, w2, w0]
    if mv is None:
        body = _vw0_kernel
    else:
        m1, m2, v0 = mv
        lm = m1.shape[1]
        body = _vw_kernel
        in_specs += [_const_spec((d2, lm)), _const_spec((lm, w)), _const_spec((1, w)),
                     _row_spec(tm, w)]
        args += [m1, m2, v0, vfirst]
    return pl.pallas_call(
        body,
        grid=(t // tm,),
        in_specs=in_specs,
        out_specs=[_row_spec(tm, w)] * 2,
        out_shape=[out] * 2,
        compiler_params=_cparams("parallel"),
        name="rwkv_vw",
    )(*args)


def _to_lanes(blk):
    r = jnp.concatenate([blk[s * 8:(s + 1) * 8, j * LANES:(j + 1) * LANES]
                         for s in range(2) for j in range(8)], axis=0)
    tr = r.T
    lo, hi = tr[0:64], tr[64:128]
    lane = lax.broadcasted_iota(jnp.int32, (64, LANES), 1)
    first = lane < 64
    return (jnp.where(first, lo, pltpu.roll(hi, 64, 1)),
            jnp.where(first, pltpu.roll(lo, 64, 1), hi))


def _from_lanes(o0, o1):
    lane = lax.broadcasted_iota(jnp.int32, (64, LANES), 1)
    first = lane < 64
    lo = jnp.where(first, o0, pltpu.roll(o1, 64, 1))
    hi = jnp.where(first, pltpu.roll(o0, 64, 1), o1)
    r = jnp.concatenate([lo, hi], axis=0).T
    rows = [jnp.concatenate([r[s * 64 + j * 8:s * 64 + j * 8 + 8] for j in range(8)], axis=1)
            for s in range(2)]
    return jnp.concatenate(rows, axis=0)


_WR, _WW, _WK, _WV, _WA, _WB = range(6)


def _wkv_kernel(r_ref, w_ref, k_ref, v_ref, kk_ref, ar_ref, rk_ref, lng_ref, lnb_ref,
                y_ref, st_ref, buf_a, buf_b, y_a, y_b, *, steps):
    n = RW_HEAD_DIM
    pairs = steps // 2

    @pl.when(pl.program_id(0) == 0)
    def _():
        st_ref[...] = jnp.zeros_like(st_ref)

    def load_pair(sp, buf):
        rows = pl.ds(pl.multiple_of(sp * 16, 16), 16)
        for idx, src in ((_WR, r_ref), (_WW, w_ref), (_WK, k_ref), (_WV, v_ref)):
            o0, o1 = _to_lanes(src[rows, :])
            buf[idx, 0] = o0
            buf[idx, 1] = o1
        kk0, kk1 = _to_lanes(kk_ref[rows, :])
        ar0, ar1 = _to_lanes(ar_ref[rows, :])
        for off, kk, ar in ((0, kk0, ar0), (1, kk1, ar1)):
            ss = jnp.sum(kk * kk, axis=0, keepdims=True)
            kkn = kk * lax.rsqrt(jnp.maximum(ss, 1e-24))
            buf[_WA, off] = -kkn
            buf[_WB, off] = kkn * ar

    def step(buf, off, nbuf, noff, ybuf, sa):
        nxt = []
        hv = n // 4
        for half in range(n // hv):
            vs = slice(half * hv, (half + 1) * hv)
            sa_h = sa[vs]
            vv = buf[_WV, off, vs, :]
            y = jnp.zeros((hv, LANES), F32)
            san = jnp.zeros((hv, LANES), F32)
            for k in range(n):
                row = slice(k, k + 1)
                sk = (st_ref[k, vs, :] * buf[_WW, off, row, :] + sa_h * buf[_WB, off, row, :]
                      + vv * buf[_WK, off, row, :])
                st_ref[k, vs, :] = sk
                y = y + sk * buf[_WR, off, row, :]
                san = san + sk * nbuf[_WA, noff, row, :]
            ybuf[off, vs, :] = y
            nxt.append(san)
        return jnp.concatenate(nxt, axis=0)

    rk = rk_ref[...]
    lng = lng_ref[...]
    lnb = lnb_ref[...]

    def finish(buf, off, ybuf):
        y = ybuf[off]
        mu = jnp.mean(y, axis=0, keepdims=True)
        yc = y - mu
        var = jnp.mean(yc * yc, axis=0, keepdims=True)
        yn = yc * lax.rsqrt(var + RW_LN_EPS) * lng + lnb
        bonus = jnp.sum(buf[_WR, off] * buf[_WK, off] * rk, axis=0, keepdims=True)
        return yn + bonus * buf[_WV, off]

    def store_pair(sp, buf, ybuf):
        rows = pl.ds(pl.multiple_of(sp * 16, 16), 16)
        y_ref[rows, :] = _from_lanes(finish(buf, 0, ybuf), finish(buf, 1, ybuf))

    load_pair(0, buf_a)
    sa0 = jnp.zeros((n, LANES), F32)
    for k in range(n):
        sa0 = sa0 + st_ref[k] * buf_a[_WA, 0, k:k + 1, :]

    def two_pairs(q, sa):
        sa = step(buf_a, 0, buf_a, 1, y_a, sa)
        load_pair(2 * q + 1, buf_b)
        sa = step(buf_a, 1, buf_b, 0, y_a, sa)
        sa = step(buf_b, 0, buf_b, 1, y_b, sa)
        store_pair(2 * q, buf_a, y_a)
        load_pair(jnp.minimum(2 * q + 2, pairs - 1), buf_a)
        sa = step(buf_b, 1, buf_a, 0, y_b, sa)
        store_pair(2 * q + 1, buf_b, y_b)
        return sa

    lax.fori_loop(0, pairs // 2, two_pairs, sa0)


def _wkv_call(r, w, k, v, kkraw, ar, rk_l, lng_l, lnb_l, batch, steps):
    t, width = r.shape
    rows = steps * batch
    n = RW_HEAD_DIM
    assert steps % 4 == 0
    pair_buf = pltpu.VMEM((6, 2, n, LANES), F32)
    pair_out = pltpu.VMEM((2, n, LANES), F32)
    return pl.pallas_call(
        functools.partial(_wkv_kernel, steps=steps),
        grid=(t // rows,),
        in_specs=[_row_spec(rows, width)] * 6 + [_const_spec((n, LANES))] * 3,
        out_specs=_row_spec(rows, width),
        out_shape=jax.ShapeDtypeStruct((t, width), F32),
        scratch_shapes=[pltpu.VMEM((n, n, LANES), F32), pair_buf, pair_buf, pair_out, pair_out],
        compiler_params=_cparams("arbitrary"),
        name="wkv7_scan",
    )(r, w, k, v, kkraw, ar, rk_l, lng_l, lnb_l)


def _head_param_lanes(pv, batch):
    a = pv.reshape(RW_HEADS // 2, 2, RW_HEAD_DIM).transpose(2, 1, 0)
    a = jnp.broadcast_to(a[..., None], a.shape + (batch,))
    return a.reshape(RW_HEAD_DIM, LANES).astype(F32)


def _gmlp_kernel(nx_ref, wu_ref, wg_ref, lng_ref, lnb_ref, ws_ref, bias_ref, yb_ref):
    nx = nx_ref[...]
    u = _gelu(_dot(nx, wu_ref[...]))
    gv = _gelu(_dot(nx, wg_ref[...]))
    mu = jnp.mean(gv, axis=-1, keepdims=True)
    gc = gv - mu
    var = jnp.mean(gc * gc, axis=-1, keepdims=True)
    v = (gc * lax.rsqrt(var + GM_LN_EPS) * lng_ref[...] + lnb_ref[...]).astype(BF16)
    gd = v.shape[1] // GM_GROUPS
    for g in range(GM_GROUPS):
        cols = slice(g * gd, (g + 1) * gd)
        s = _dot(ws_ref[g], v[:, cols]) + bias_ref[:, g:g + 1]
        yb_ref[:, cols] = (u[:, cols] * s).astype(BF16)


def _gmlp_call(nx, wu, wg, lng, lnb, ws_k, bias):
    t, d2 = nx.shape
    w = wu.shape[1]
    rows = ws_k.shape[1]
    return pl.pallas_call(
        _gmlp_kernel,
        grid=(t // rows,),
        in_specs=[_row_spec(rows, d2), _const_spec((d2, w)), _const_spec((d2, w)),
                  _const_spec((1, w)), _const_spec((1, w)),
                  _const_spec(ws_k.shape), _const_spec((rows, GM_GROUPS))],
        out_specs=_row_spec(rows, w),
        out_shape=jax.ShapeDtypeStruct((t, w), BF16),
        compiler_params=_cparams("parallel"),
        name="gmlp",
    )(nx, wu, wg, lng, lnb, ws_k, bias)


def _merge_kernel(nx_ref, y_ref, yb_ref, h_ref, wg_ref, g2_ref, wga_ref, wgb_ref,
                  wa_ref, wb_ref, wo_ref, nf_ref, hn_ref, n2_ref):
    nx = nx_ref[...]
    gate = _dot(_sigmoid(_dot(nx, wg_ref[...])).astype(BF16), g2_ref[...])
    ya = (y_ref[...] * gate).astype(BF16)
    pa = _dot(ya, wa_ref[...])
    pb = _dot(yb_ref[...], wb_ref[...])
    merged = _sigmoid(_dot(nx, wga_ref[...])) * pa + _sigmoid(_dot(nx, wgb_ref[...])) * pb
    hn = h_ref[...] + _dot(merged.astype(BF16), wo_ref[...])
    hn_ref[...] = hn
    n2_ref[...] = _rms(hn, nf_ref[...]).astype(BF16)


def _merge_call(nx, y, yb, h, wg, g2, wga, wgb, wa, wb, wo, nf, tm):
    t, d2 = nx.shape
    d = h.shape[1]
    w = y.shape[1]
    lg = wg.shape[1]
    return pl.pallas_call(
        _merge_kernel,
        grid=(t // tm,),
        in_specs=[_row_spec(tm, d2), _row_spec(tm, w), _row_spec(tm, w), _row_spec(tm, d),
                  _const_spec((d2, lg)), _const_spec((lg, w)),
                  _const_spec((d2, d)), _const_spec((d2, d)),
                  _const_spec((w, d)), _const_spec((w, d)), _const_spec((d, d)),
                  _const_spec((1, d))],
        out_specs=[_row_spec(tm, d), _row_spec(tm, d)],
        out_shape=[jax.ShapeDtypeStruct((t, d), F32), jax.ShapeDtypeStruct((t, d), BF16)],
        compiler_params=_cparams("parallel"),
        name="merge",
    )(nx, y, yb, h, wg, g2, wga, wgb, wa, wb, wo, nf)


def _bitonic_exchanges(n):
    out = []
    k = 2
    while k <= n:
        j = k // 2
        while j >= 1:
            for i in range(n):
                l = i ^ j
                if l > i:
                    out.append((i, l, (i & k) == 0))
            j //= 2
        k *= 2
    return out


def _top_values(s, count, rows):
    slabs = [s[i:i + SUBLANES] for i in range(0, s.shape[0], SUBLANES)]
    wires = 1
    while wires < len(slabs):
        wires *= 2
    slabs += [None] * (wires - len(slabs))
    for i, j, ascending in _bitonic_exchanges(wires):
        a, b = slabs[i], slabs[j]
        if a is None and b is None:
            continue
        if a is None or b is None:
            hi, lo = (a if b is None else b), None
        else:
            hi, lo = jnp.maximum(a, b), jnp.minimum(a, b)
        slabs[i], slabs[j] = (lo, hi) if ascending else (hi, lo)
    lists = slabs[::-1]
    neg = jnp.full((SUBLANES, s.shape[1]), -jnp.inf, s.dtype)
    lists = [neg if x is None else x for x in lists]
    tops = []
    for it in range(count):
        m = jnp.max(lists[0], axis=0, keepdims=True)
        tops.append(m)
        hit = lists[0] == m
        depth = min(count - it - 1, len(lists))
        for j in range(depth):
            below = lists[j + 1] if j + 1 < len(lists) else neg
            lists[j] = jnp.where(hit, below, lists[j])
    tops += [jnp.full_like(tops[0], -jnp.inf)] * (rows - count)
    return jnp.concatenate(tops, axis=0)


def _peer_select_kernel(n2_ref, h_ref, p_ref, wq_ref, keys_ref, wple_ref, wpg_ref,
                        take_ref, rank_ref, e1_ref, e2_ref, hp_ref, q_scr):
    n2 = n2_ref[...]
    ple = _dot(p_ref[...], wple_ref[...]) * _sigmoid(_dot(n2, wpg_ref[...]))
    hp_ref[...] = h_ref[...] + ple
    q = _dot(n2, wq_ref[...]).astype(BF16)
    for hp in range(2 * PEER_HEADS):
        q_scr[hp] = q[:, hp * PEER_HALF:(hp + 1) * PEER_HALF]
    tt = n2.shape[0]
    kk = PEER_TOPK
    nt = (((1,), (1,)), ((), ()))

    def head(h, _):
        s1 = lax.dot_general(keys_ref[h, 0], q_scr[2 * h], nt, preferred_element_type=F32)
        s2 = lax.dot_general(keys_ref[h, 1], q_scr[2 * h + 1], nt, preferred_element_type=F32)
        def chunk_stats(c):
            cols = slice(c * LANES, (c + 1) * LANES)
            t1 = _top_values(s1[:, cols], kk + 1, 24)
            t2 = _top_values(s2[:, cols], kk + 1, 24)
            cand = [t1[0:1] + t2]
            cand += [t1[i:i + 1] + t2[0:8] for i in range(1, 8)]
            cand += [t1[8:24] + t2[0:1]]
            best = _top_values(jnp.concatenate(cand, axis=0), kk + 1, kk + 1)
            top = best[0:kk]
            inv_z = 1.0 / jnp.sum(jnp.exp(top - top[0:1]), axis=0, keepdims=True)
            tau = 0.5 * (best[kk - 1:kk] + best[kk:kk + 1])
            return tau, t1[0:1], t2[0:kk], inv_z

        group = 2
        per = PEER_CHUNK // LANES
        for c0 in range(0, tt // LANES, group):
            stats = [chunk_stats(c) for c in range(c0, c0 + group)]
            for c, (tau, m1, top2, inv_z) in zip(range(c0, c0 + group), stats):
                cols = slice(c * LANES, (c + 1) * LANES)
                dst = (h, c // per, slice(None), slice((c % per) * LANES, (c % per + 1) * LANES))
                s1c = s1[:, cols]
                s2c = s2[:, cols]
                thr = tau - s1c
                take = jnp.zeros_like(s1c)
                rank = jnp.zeros_like(s2c)
                for j in range(kk):
                    tj = top2[j:j + 1]
                    take = take + jnp.where(tj >= thr, 1.0, 0.0)
                    rank = rank + jnp.where(tj > s2c, 1.0, 0.0)
                take_ref[dst] = take
                rank_ref[dst] = rank.astype(BF16)
                e1_ref[dst] = jnp.exp(s1c - m1)
                e2_ref[dst] = (jnp.exp(s2c - top2[0:1]) * (0.5 * inv_z)).astype(BF16)
        return 0

    lax.fori_loop(0, PEER_HEADS, head, 0)


def _peer_select_call(n2, h, p, wq, keys, wple, wpg, tt):
    t, d = n2.shape
    pd = p.shape[1]
    dq = wq.shape[1]
    nh, nk = PEER_HEADS, PEER_KEYS
    tspec = pl.BlockSpec((nh, tt // PEER_CHUNK, nk, PEER_CHUNK), lambda i: (0, i, 0, 0))
    oshape = (nh, t // PEER_CHUNK, nk, PEER_CHUNK)
    tshapes = [jax.ShapeDtypeStruct(oshape, dt) for dt in (F32, BF16, F32, BF16)]
    return pl.pallas_call(
        _peer_select_kernel,
        grid=(t // tt,),
        in_specs=[_row_spec(tt, d), _row_spec(tt, d), _row_spec(tt, pd),
                  _const_spec((d, dq)), _const_spec(keys.shape),
                  _const_spec((pd, d)), _const_spec((d, d))],
        out_specs=[tspec] * 4 + [_row_spec(tt, d)],
        out_shape=tshapes + [jax.ShapeDtypeStruct((t, d), F32)],
        scratch_shapes=[pltpu.VMEM((2 * nh, tt, PEER_HALF), BF16)],
        compiler_params=_cparams("parallel"),
        name="peer_select",
    )(n2, h, p, wq, keys, wple, wpg)


def _peer_dense_kernel(n2_ref, u_ref, vt_ref, take_ref, e1_ref, rank_ref, e2_ref, hp_ref,
                       out_ref, acc_ref, act0_ref, act1_ref, ga0_ref, ga1_ref, *, nj):
    f = pl.program_id(0)
    nt = (((1,), (1,)), ((), ()))
    nc, te, cw = act0_ref.shape
    nk = PEER_KEYS
    rb = 2 * SUBLANES
    j3 = jnp.maximum(f - 2, 0) % nj

    @pl.when(f == 0)
    def _():
        act1_ref[...] = jnp.zeros_like(act1_ref)
        ga0_ref[...] = jnp.zeros_like(ga0_ref)

    @pl.when(j3 == 0)
    def _():
        acc_ref[...] = jnp.zeros_like(acc_ref)

    def stages(act_new, act_cur, ga_cur, ga_old):
        for c in range(nc):
            tok = slice(c * cw, (c + 1) * cw)
            act_new[c] = lax.dot_general(u_ref[...], n2_ref[tok, :], nt,
                                         preferred_element_type=F32)
            for i1 in range(te // nk):
                def row16(ref, h):
                    return jnp.broadcast_to(ref[h, c, i1:i1 + 1, :], (rb, cw)).astype(BF16)

                take = [row16(take_ref, h) for h in range(PEER_HEADS)]
                e1 = [row16(e1_ref, h) for h in range(PEER_HEADS)]
                zero = jnp.zeros((rb, cw), BF16)
                for r in range(nk // rb):
                    rs = slice(r * rb, (r + 1) * rb)
                    g = None
                    for h in range(PEER_HEADS):
                        keep = rank_ref[h, c, rs, :] < take[h]
                        term = jnp.where(keep, e2_ref[h, c, rs, :], zero) * e1[h]
                        g = term if g is None else g + term
                    rows = slice(i1 * nk + r * rb, i1 * nk + (r + 1) * rb)
                    ga_cur[c, rows, :] = g * _gelu_x2(act_cur[c, rows, :].astype(BF16))
            acc_ref[c] += _dot(vt_ref[...], ga_old[c])

    @pl.when(f % 2 == 0)
    def _():
        stages(act0_ref, act1_ref, ga1_ref, ga0_ref)

    @pl.when(f % 2 == 1)
    def _():
        stages(act1_ref, act0_ref, ga0_ref, ga1_ref)

    @pl.when(jnp.logical_and(f >= 2, j3 == nj - 1))
    def _():
        for c in range(nc):
            out_ref[c * cw:(c + 1) * cw, :] = hp_ref[c * cw:(c + 1) * cw, :] + acc_ref[c].T


def _peer_dense_call(n2, u, vt, take, e1, rank, e2, hp, tt, te):
    t, d = n2.shape
    ne = u.shape[0]
    nh, nk = PEER_HEADS, PEER_KEYS
    r1 = te // nk
    cw = PEER_CHUNK
    nc = tt // cw
    ni, nj = t // tt, ne // te
    last = ni * nj - 1

    def pair(f, lag):
        a = jnp.clip(f - lag, 0, last)
        return a // nj, a % nj

    score_blk = (nh, nc, nk, cw)
    row_blk = (nh, nc, r1, cw)
    return pl.pallas_call(
        functools.partial(_peer_dense_kernel, nj=nj),
        grid=(ni * nj + 2,),
        in_specs=[pl.BlockSpec((tt, d), lambda f: (pair(f, 0)[0], 0)),
                  pl.BlockSpec((te, d), lambda f: (pair(f, 0)[1], 0)),
                  pl.BlockSpec((d, te), lambda f: (0, pair(f, 2)[1])),
                  pl.BlockSpec(row_blk, lambda f: (0, pair(f, 1)[0], pair(f, 1)[1], 0)),
                  pl.BlockSpec(row_blk, lambda f: (0, pair(f, 1)[0], pair(f, 1)[1], 0)),
                  pl.BlockSpec(score_blk, lambda f: (0, pair(f, 1)[0], 0, 0)),
                  pl.BlockSpec(score_blk, lambda f: (0, pair(f, 1)[0], 0, 0)),
                  pl.BlockSpec((tt, d), lambda f: (pair(f, 2)[0], 0))],
        out_specs=pl.BlockSpec((tt, d), lambda f: (pair(f, 2)[0], 0)),
        out_shape=jax.ShapeDtypeStruct((t, d), F32),
        scratch_shapes=[pltpu.VMEM((nc, d, cw), F32)]
        + [pltpu.VMEM((nc, te, cw), F32)] * 2 + [pltpu.VMEM((nc, te, cw), BF16)] * 2,
        compiler_params=_cparams("arbitrary"),
        name="peer_dense",
    )(n2, u, vt, take, e1, rank, e2, hp)


def _final_norm_kernel(h_ref, g_ref, o_ref):
    o_ref[0] = _rms(h_ref[...], g_ref[...])


def _final_norm_call(h2d, g, batch, seq, ts):
    d = g.shape[-1]
    return pl.pallas_call(
        _final_norm_kernel,
        grid=(seq // ts, batch),
        in_specs=[pl.BlockSpec((ts, d), lambda s, b: (s, b)), _const_spec((1, d))],
        out_specs=pl.BlockSpec((1, ts, d), lambda s, b: (b, s, 0)),
        out_shape=jax.ShapeDtypeStruct((batch, seq, d), F32),
        compiler_params=_cparams("parallel", "parallel"),
        name="final_norm",
    )(h2d, g.reshape(1, d))


def _tile(total, want):
    want = min(want, total)
    assert total % want == 0, (total, want)
    return want


def kernel(x, p, norm_mix, norm_ffn, norm_final, w_in, rw_w0, rw_w2, rw_a0, rw_a2, rw_g2, rw_kk, rw_ka, rw_rk, rw_ln_g, rw_ln_b, rw_mv_w1, rw_mv_w2, rw_mv_v0, gm_ln_g, gm_ln_b, gm_ws, gm_bs, w_proj_a, w_proj_b, w_out, peer_wq, peer_keys, peer_u, peer_v, ple_w, ple_gate):
    batch, seq, d = x.shape
    depth = w_in.shape[0]
    t = batch * seq
    w = RW_HEADS * RW_HEAD_DIM
    assert batch * RW_HEADS == LANES and batch == SUBLANES and d == w
    assert seq % GM_CHUNK == 0

    tm = _tile(t, 512)
    tt = _tile(t, 512)
    te = 1024
    steps = _tile(seq, 64)

    gw = gm_ln_g.shape[1]
    sizes = (w, w, w, rw_w2.shape[1], rw_a2.shape[1], rw_g2.shape[1], gw, gw, d, d)
    offs = [0]
    for sz in sizes:
        offs.append(offs[-1] + sz)

    def col(wi, idx):
        return wi[:, offs[idx]:offs[idx + 1]].astype(BF16)

    h = x.transpose(1, 0, 2).reshape(t, d)
    pt = p.transpose(0, 2, 1, 3).reshape(depth, t, p.shape[-1]).astype(BF16)
    row = lambda a: a.reshape(1, -1).astype(F32)
    expand = jnp.repeat(jnp.eye(GM_CHUNK, dtype=F32), batch, axis=0)
    same_batch = jnp.tile(jnp.eye(batch, dtype=F32), (GM_CHUNK, GM_CHUNK))
    tril = jnp.tril(jnp.ones((GM_CHUNK, GM_CHUNK), F32))

    v_first = None
    for i in range(depth):
        wi = w_in[i]
        nx = _norm_shift(h, norm_mix[i], batch, tm)

        r, k, kkraw, ar = _rk_call(nx, col(wi, 0), col(wi, 1), col(wi, 4), rw_a2[i].astype(BF16),
                                   row(rw_a0[i]), row(rw_kk[i]), row(rw_ka[i]), tm)
        mv = None
        if i > 0:
            mv = (rw_mv_w1[i - 1].astype(BF16), rw_mv_w2[i - 1].astype(BF16), row(rw_mv_v0[i - 1]))
        v, dec = _vw_call(nx, col(wi, 2), col(wi, 3), rw_w2[i].astype(BF16), row(rw_w0[i]), mv,
                          v_first, tm)
        if i == 0:
            v_first = v

        y = _wkv_call(r, dec, k, v, kkraw, ar,
                      _head_param_lanes(rw_rk[i], batch),
                      _head_param_lanes(rw_ln_g[i].reshape(RW_HEADS, RW_HEAD_DIM), batch),
                      _head_param_lanes(rw_ln_b[i].reshape(RW_HEADS, RW_HEAD_DIM), batch),
                      batch, steps)

        ws_k = (jnp.einsum("ri,gij,cj->grc", expand, gm_ws[i] * tril, expand)
                * same_batch).astype(BF16)
        bias = jnp.repeat(gm_bs[i].T, batch, axis=0).astype(F32)
        yb = _gmlp_call(nx, col(wi, 6), col(wi, 7), row(gm_ln_g[i]), row(gm_ln_b[i]), ws_k, bias)

        h, n2 = _merge_call(nx, y, yb, h, col(wi, 5), rw_g2[i].astype(BF16), col(wi, 8), col(wi, 9),
                            w_proj_a[i].astype(BF16), w_proj_b[i].astype(BF16),
                            w_out[i].astype(BF16), row(norm_ffn[i]), tm)

        take, rank, e1, e2, hp = _peer_select_call(
            n2, h, pt[i], peer_wq[i].astype(BF16), peer_keys[i].astype(BF16),
            ple_w[i].astype(BF16), ple_gate[i].astype(BF16), tt)
        h = _peer_dense_call(n2, peer_u[i].astype(BF16), peer_v[i].T.astype(BF16),
                             take, e1, rank, e2, hp, tt, te)

    return _final_norm_call(h.reshape(seq, batch * d), norm_final, batch, seq, _tile(seq, 512))
```

```python
import functools
import math

import jax
import jax.numpy as jnp
from jax import lax
from jax.experimental import pallas as pl
from jax.experimental.pallas import tpu as pltpu

F32 = jnp.float32
BF16 = jnp.bfloat16

RW_HEADS = 16
RW_HEAD_DIM = 64
RW_LN_EPS = 64e-5
GM_GROUPS = 8
GM_CHUNK = 128
GM_LN_EPS = 1e-5
PEER_HEADS = 8
PEER_KEYS = 128
PEER_HALF = 128
PEER_TOPK = 16
PEER_CHUNK = 256
RMS_EPS = 1e-6

LANES = 128
SUBLANES = 8
VMEM_LIMIT = 56 * 1024 * 1024


def _cparams(*sem):
    return pltpu.CompilerParams(dimension_semantics=tuple(sem), vmem_limit_bytes=VMEM_LIMIT)


def _const_spec(shape):
    nd = len(shape)
    return pl.BlockSpec(shape, lambda *_: (0,) * nd, pipeline_mode=pl.Buffered(1))


def _row_spec(rows, cols):
    return pl.BlockSpec((rows, cols), lambda i: (i, 0))


def _gelu(x):
    c = math.sqrt(2.0 / math.pi)
    return 0.5 * x * (1.0 + jnp.tanh(c * (x + 0.044715 * (x * x * x))))


def _gelu_x2(x):
    c = math.sqrt(2.0 / math.pi)
    return x * (1.0 + jnp.tanh(x * (c + (0.044715 * c) * (x * x))))


def _sigmoid(x):
    return 1.0 / (1.0 + jnp.exp(-x))


def _dot(a, b):
    return jnp.dot(a, b, preferred_element_type=F32)


def _rms(x, g):
    return x * lax.rsqrt(jnp.mean(x * x, axis=-1, keepdims=True) + RMS_EPS) * g


def _norm_shift_kernel(h_ref, hprev_ref, g_ref, nx_ref, *, batch):
    i = pl.program_id(0)
    g = g_ref[...]
    n = _rms(h_ref[...], g)
    prev = _rms(hprev_ref[...], g) * (i > 0).astype(F32)
    d = n.shape[1]
    shifted = jnp.concatenate([prev, n[:-batch]], axis=0)
    nx_ref[:, :d] = n.astype(BF16)
    nx_ref[:, d:] = shifted.astype(BF16)


def _norm_shift(h, g, batch, tm):
    t, d = h.shape
    per = tm // batch
    return pl.pallas_call(
        functools.partial(_norm_shift_kernel, batch=batch),
        grid=(t // tm,),
        in_specs=[_row_spec(tm, d),
                  pl.BlockSpec((batch, d), lambda i: (jnp.maximum(i * per - 1, 0), 0)),
                  _const_spec((1, d))],
        out_specs=_row_spec(tm, 2 * d),
        out_shape=jax.ShapeDtypeStruct((t, 2 * d), BF16),
        compiler_params=_cparams("parallel"),
        name="norm_shift",
    )(h, h, g.reshape(1, d))


def _rk_kernel(nx_ref, wr_ref, wk_ref, wa_ref, a2_ref, a0_ref, kk_ref, ka_ref,
               r_ref, k_ref, kkraw_ref, ar_ref):
    nx = nx_ref[...]
    r_ref[...] = _dot(nx, wr_ref[...])
    zk = _dot(nx, wk_ref[...])
    za = _dot(nx, wa_ref[...])
    a_rate = _sigmoid(a0_ref[...] + _dot(za.astype(BF16), a2_ref[...]))
    ar_ref[...] = a_rate
    kkraw_ref[...] = zk * kk_ref[...]
    k_ref[...] = zk * (1.0 + (a_rate - 1.0) * ka_ref[...])


def _rk_call(nx, wr, wk, wa, a2, a0, kk, ka, tm):
    t, d2 = nx.shape
    w = wr.shape[1]
    la = wa.shape[1]
    out = jax.ShapeDtypeStruct((t, w), F32)
    return pl.pallas_call(
        _rk_kernel,
        grid=(t // tm,),
        in_specs=[_row_spec(tm, d2), _const_spec((d2, w)), _const_spec((d2, w)),
                  _const_spec((d2, la)), _const_spec((la, w)),
                  _const_spec((1, w)), _const_spec((1, w)), _const_spec((1, w))],
        out_specs=[_row_spec(tm, w)] * 4,
        out_shape=[out] * 4,
        compiler_params=_cparams("parallel"),
        name="rwkv_rk",
    )(nx, wr, wk, wa, a2, a0, kk, ka)


def _decay_from(nx, ww_ref, w2_ref, w0_ref):
    zw = _dot(nx, ww_ref[...])
    x = -(w0_ref[...] + _dot(jnp.tanh(zw).astype(BF16), w2_ref[...]))
    softplus = jnp.maximum(x, 0.0) + jnp.log(1.0 + jnp.exp(-jnp.abs(x)))
    return jnp.exp(-jnp.exp(-softplus - 0.5))


def _vw0_kernel(nx_ref, wv_ref, ww_ref, w2_ref, w0_ref, v_ref, dec_ref):
    nx = nx_ref[...]
    v_ref[...] = _dot(nx, wv_ref[...])
    dec_ref[...] = _decay_from(nx, ww_ref, w2_ref, w0_ref)


def _vw_kernel(nx_ref, wv_ref, ww_ref, w2_ref, w0_ref, m1_ref, m2_ref, v0_ref, vfirst_ref,
               v_ref, dec_ref):
    nx = nx_ref[...]
    zv = _dot(nx, wv_ref[...])
    lo = _dot(nx, m1_ref[...])
    vmix = _sigmoid(v0_ref[...] + _dot(lo.astype(BF16), m2_ref[...]))
    v_ref[...] = zv + (vfirst_ref[...] - zv) * vmix
    dec_ref[...] = _decay_from(nx, ww_ref, w2_ref, w0_ref)


def _vw_call(nx, wv, ww, w2, w0, mv, vfirst, tm):
    t, d2 = nx.shape
    w = wv.shape[1]
    lw = ww.shape[1]
    out = jax.ShapeDtypeStruct((t, w), F32)
    in_specs = [_row_spec(tm, d2), _const_spec((d2, w)), _const_spec((d2, lw)),
                _const_spec((lw, w)), _const_spec((1, w))]
    args = [nx, wv, ww, w2, w0]
    if mv is None:
        body = _vw0_kernel
    else:
        m1, m2, v0 = mv
        lm = m1.shape[1]
        body = _vw_kernel
        in_specs += [_const_spec((d2, lm)), _const_spec((lm, w)), _const_spec((1, w)),
                     _row_spec(tm, w)]
        args += [m1, m2, v0, vfirst]
    return pl.pallas_call(
        body,
        grid=(t // tm,),
        in_specs=in_specs,
        out_specs=[_row_spec(tm, w)] * 2,
        out_shape=[out] * 2,
        compiler_params=_cparams("parallel"),
        name="rwkv_vw",
    )(*args)


def _to_lanes(blk):
    r = jnp.concatenate([blk[s * 8:(s + 1) * 8, j * LANES:(j + 1) * LANES]
                         for s in range(2) for j in range(8)], axis=0)
    tr = r.T
    lo, hi = tr[0:64], tr[64:128]
    lane = lax.broadcasted_iota(jnp.int32, (64, LANES), 1)
    first = lane < 64
    return (jnp.where(first, lo, pltpu.roll(hi, 64, 1)),
            jnp.where(first, pltpu.roll(lo, 64, 1), hi))


def _from_lanes(o0, o1):
    lane = lax.broadcasted_iota(jnp.int32, (64, LANES), 1)
    first = lane < 64
    lo = jnp.where(first, o0, pltpu.roll(o1, 64, 1))
    hi = jnp.where(first, pltpu.roll(o0, 64, 1), o1)
    r = jnp.concatenate([lo, hi], axis=0).T
    rows = [jnp.concatenate([r[s * 64 + j * 8:s * 64 + j * 8 + 8] for j in range(8)], axis=1)
            for s in range(2)]
    return jnp.concatenate(rows, axis=0)


_WR, _WW, _WK, _WV, _WA, _WB = range(6)


def _wkv_kernel(r_ref, w_ref, k_ref, v_ref, kk_ref, ar_ref, rk_ref, lng_ref, lnb_ref,
                y_ref, st_ref, buf_a, buf_b, y_a, y_b, *, steps):
    n = RW_HEAD_DIM
    pairs = steps // 2

    @pl.when(pl.program_id(0) == 0)
    def _():
        st_ref[...] = jnp.zeros_like(st_ref)

    def load_pair(sp, buf):
        rows = pl.ds(pl.multiple_of(sp * 16, 16), 16)
        for idx, src in ((_WR, r_ref), (_WW, w_ref), (_WK, k_ref), (_WV, v_ref)):
            o0, o1 = _to_lanes(src[rows, :])
            buf[idx, 0] = o0
            buf[idx, 1] = o1
        kk0, kk1 = _to_lanes(kk_ref[rows, :])
        ar0, ar1 = _to_lanes(ar_ref[rows, :])
        for off, kk, ar in ((0, kk0, ar0), (1, kk1, ar1)):
            ss = jnp.sum(kk * kk, axis=0, keepdims=True)
            kkn = kk * lax.rsqrt(jnp.maximum(ss, 1e-24))
            buf[_WA, off] = -kkn
            buf[_WB, off] = kkn * ar

    def step(buf, off, nbuf, noff, ybuf, sa):
        nxt = []
        hv = n // 4
        for half in range(n // hv):
            vs = slice(half * hv, (half + 1) * hv)
            sa_h = sa[vs]
            vv = buf[_WV, off, vs, :]
            y = jnp.zeros((hv, LANES), F32)
            san = jnp.zeros((hv, LANES), F32)
            for k in range(n):
                row = slice(k, k + 1)
                sk = (st_ref[k, vs, :] * buf[_WW, off, row, :] + sa_h * buf[_WB, off, row, :]
                      + vv * buf[_WK, off, row, :])
                st_ref[k, vs, :] = sk
                y = y + sk * buf[_WR, off, row, :]
                san = san + sk * nbuf[_WA, noff, row, :]
            ybuf[off, vs, :] = y
            nxt.append(san)
        return jnp.concatenate(nxt, axis=0)

    rk = rk_ref[...]
    lng = lng_ref[...]
    lnb = lnb_ref[...]

    def finish(buf, off, ybuf):
        y = ybuf[off]
        mu = jnp.mean(y, axis=0, keepdims=True)
        yc = y - mu
        var = jnp.mean(yc * yc, axis=0, keepdims=True)
        yn = yc * lax.rsqrt(var + RW_LN_EPS) * lng + lnb
        bonus = jnp.sum(buf[_WR, off] * buf[_WK, off] * rk, axis=0, keepdims=True)
        return yn + bonus * buf[_WV, off]

    def store_pair(sp, buf, ybuf):
        rows = pl.ds(pl.multiple_of(sp * 16, 16), 16)
        y_ref[rows, :] = _from_lanes(finish(buf, 0, ybuf), finish(buf, 1, ybuf))

    load_pair(0, buf_a)
    sa0 = jnp.zeros((n, LANES), F32)
    for k in range(n):
        sa0 = sa0 + st_ref[k] * buf_a[_WA, 0, k:k + 1, :]

    def two_pairs(q, sa):
        sa = step(buf_a, 0, buf_a, 1, y_a, sa)
        load_pair(2 * q + 1, buf_b)
        sa = step(buf_a, 1, buf_b, 0, y_a, sa)
        sa = step(buf_b, 0, buf_b, 1, y_b, sa)
        store_pair(2 * q, buf_a, y_a)
        load_pair(jnp.minimum(2 * q + 2, pairs - 1), buf_a)
        sa = step(buf_b, 1, buf_a, 0, y_b, sa)
        store_pair(2 * q + 1, buf_b, y_b)
        return sa

    lax.fori_loop(0, pairs // 2, two_pairs, sa0)


def _wkv_call(r, w, k, v, kkraw, ar, rk_l, lng_l, lnb_l, batch, steps):
    t, width = r.shape
    rows = steps * batch
    n = RW_HEAD_DIM
    assert steps % 4 == 0
    pair_buf = pltpu.VMEM((6, 2, n, LANES), F32)
    pair_out = pltpu.VMEM((2, n, LANES), F32)
    return pl.pallas_call(
        functools.partial(_wkv_kernel, steps=steps),
        grid=(t // rows,),
        in_specs=[_row_spec(rows, width)] * 6 + [_const_spec((n, LANES))] * 3,
        out_specs=_row_spec(rows, width),
        out_shape=jax.ShapeDtypeStruct((t, width), F32),
        scratch_shapes=[pltpu.VMEM((n, n, LANES), F32), pair_buf, pair_buf, pair_out, pair_out],
        compiler_params=_cparams("arbitrary"),
        name="wkv7_scan",
    )(r, w, k, v, kkraw, ar, rk_l, lng_l, lnb_l)


def _head_param_lanes(pv, batch):
    a = pv.reshape(RW_HEADS // 2, 2, RW_HEAD_DIM).transpose(2, 1, 0)
    a = jnp.broadcast_to(a[..., None], a.shape + (batch,))
    return a.reshape(RW_HEAD_DIM, LANES).astype(F32)


def _gmlp_kernel(nx_ref, wu_ref, wg_ref, lng_ref, lnb_ref, ws_ref, bias_ref, yb_ref):
    nx = nx_ref[...]
    u = _gelu(_dot(nx, wu_ref[...]))
    gv = _gelu(_dot(nx, wg_ref[...]))
    mu = jnp.mean(gv, axis=-1, keepdims=True)
    gc = gv - mu
    var = jnp.mean(gc * gc, axis=-1, keepdims=True)
    v = (gc * lax.rsqrt(var + GM_LN_EPS) * lng_ref[...] + lnb_ref[...]).astype(BF16)
    gd = v.shape[1] // GM_GROUPS
    for g in range(GM_GROUPS):
        cols = slice(g * gd, (g + 1) * gd)
        s = _dot(ws_ref[g], v[:, cols]) + bias_ref[:, g:g + 1]
        yb_ref[:, cols] = (u[:, cols] * s).astype(BF16)


def _gmlp_call(nx, wu, wg, lng, lnb, ws_k, bias):
    t, d2 = nx.shape
    w = wu.shape[1]
    rows = ws_k.shape[1]
    return pl.pallas_call(
        _gmlp_kernel,
        grid=(t // rows,),
        in_specs=[_row_spec(rows, d2), _const_spec((d2, w)), _const_spec((d2, w)),
                  _const_spec((1, w)), _const_spec((1, w)),
                  _const_spec(ws_k.shape), _const_spec((rows, GM_GROUPS))],
        out_specs=_row_spec(rows, w),
        out_shape=jax.ShapeDtypeStruct((t, w), BF16),
        compiler_params=_cparams("parallel"),
        name="gmlp",
    )(nx, wu, wg, lng, lnb, ws_k, bias)


def _merge_kernel(nx_ref, y_ref, yb_ref, h_ref, wg_ref, g2_ref, wga_ref, wgb_ref,
                  wa_ref, wb_ref, wo_ref, nf_ref, hn_ref, n2_ref):
    nx = nx_ref[...]
    gate = _dot(_sigmoid(_dot(nx, wg_ref[...])).astype(BF16), g2_ref[...])
    ya = (y_ref[...] * gate).astype(BF16)
    pa = _dot(ya, wa_ref[...])
    pb = _dot(yb_ref[...], wb_ref[...])
    merged = _sigmoid(_dot(nx, wga_ref[...])) * pa + _sigmoid(_dot(nx, wgb_ref[...])) * pb
    hn = h_ref[...] + _dot(merged.astype(BF16), wo_ref[...])
    hn_ref[...] = hn
    n2_ref[...] = _rms(hn, nf_ref[...]).astype(BF16)


def _merge_call(nx, y, yb, h, wg, g2, wga, wgb, wa, wb, wo, nf, tm):
    t, d2 = nx.shape
    d = h.shape[1]
    w = y.shape[1]
    lg = wg.shape[1]
    return pl.pallas_call(
        _merge_kernel,
        grid=(t // tm,),
        in_specs=[_row_spec(tm, d2), _row_spec(tm, w), _row_spec(tm, w), _row_spec(tm, d),
                  _const_spec((d2, lg)), _const_spec((lg, w)),
                  _const_spec((d2, d)), _const_spec((d2, d)),
                  _const_spec((w, d)), _const_spec((w, d)), _const_spec((d, d)),
                  _const_spec((1, d))],
        out_specs=[_row_spec(tm, d), _row_spec(tm, d)],
        out_shape=[jax.ShapeDtypeStruct((t, d), F32), jax.ShapeDtypeStruct((t, d), BF16)],
        compiler_params=_cparams("parallel"),
        name="merge",
    )(nx, y, yb, h, wg, g2, wga, wgb, wa, wb, wo, nf)


def _bitonic_exchanges(n):
    out = []
    k = 2
    while k <= n:
        j = k // 2
        while j >= 1:
            for i in range(n):
                l = i ^ j
                if l > i:
                    out.append((i, l, (i & k) == 0))
            j //= 2
        k *= 2
    return out


def _top_values(s, count, rows):
    slabs = [s[i:i + SUBLANES] for i in range(0, s.shape[0], SUBLANES)]
    wires = 1
    while wires < len(slabs):
        wires *= 2
    slabs += [None] * (wires - len(slabs))
    for i, j, ascending in _bitonic_exchanges(wires):
        a, b = slabs[i], slabs[j]
        if a is None and b is None:
            continue
        if a is None or b is None:
            hi, lo = (a if b is None else b), None
        else:
            hi, lo = jnp.maximum(a, b), jnp.minimum(a, b)
        slabs[i], slabs[j] = (lo, hi) if ascending else (hi, lo)
    lists = slabs[::-1]
    neg = jnp.full((SUBLANES, s.shape[1]), -jnp.inf, s.dtype)
    lists = [neg if x is None else x for x in lists]
    tops = []
    for it in range(count):
        m = jnp.max(lists[0], axis=0, keepdims=True)
        tops.append(m)
        hit = lists[0] == m
        depth = min(count - it - 1, len(lists))
        for j in range(depth):
            below = lists[j + 1] if j + 1 < len(lists) else neg
            lists[j] = jnp.where(hit, below, lists[j])
    tops += [jnp.full_like(tops[0], -jnp.inf)] * (rows - count)
    return jnp.concatenate(tops, axis=0)


def _peer_select_kernel(n2_ref, h_ref, p_ref, wq_ref, keys_ref, wple_ref, wpg_ref,
                        take_ref, rank_ref, e1_ref, e2_ref, hp_ref, q_scr):
    n2 = n2_ref[...]
    ple = _dot(p_ref[...], wple_ref[...]) * _sigmoid(_dot(n2, wpg_ref[...]))
    hp_ref[...] = h_ref[...] + ple
    q = _dot(n2, wq_ref[...]).astype(BF16)
    for hp in range(2 * PEER_HEADS):
        q_scr[hp] = q[:, hp * PEER_HALF:(hp + 1) * PEER_HALF]
    tt = n2.shape[0]
    kk = PEER_TOPK
    nt = (((1,), (1,)), ((), ()))

    def head(h, _):
        s1 = lax.dot_general(keys_ref[h, 0], q_scr[2 * h], nt, preferred_element_type=F32)
        s2 = lax.dot_general(keys_ref[h, 1], q_scr[2 * h + 1], nt, preferred_element_type=F32)
        def chunk_stats(c):
            cols = slice(c * LANES, (c + 1) * LANES)
            t1 = _top_values(s1[:, cols], kk + 1, 24)
            t2 = _top_values(s2[:, cols], kk + 1, 24)
            cand = [t1[0:1] + t2]
            cand += [t1[i:i + 1] + t2[0:8] for i in range(1, 8)]
            cand += [t1[8:24] + t2[0:1]]
            best = _top_values(jnp.concatenate(cand, axis=0), kk + 1, kk + 1)
            top = best[0:kk]
            inv_z = 1.0 / jnp.sum(jnp.exp(top - top[0:1]), axis=0, keepdims=True)
            tau = 0.5 * (best[kk - 1:kk] + best[kk:kk + 1])
            return tau, t1[0:1], t2[0:kk], inv_z

        group = 2
        per = PEER_CHUNK // LANES
        for c0 in range(0, tt // LANES, group):
            stats = [chunk_stats(c) for c in range(c0, c0 + group)]
            for c, (tau, m1, top2, inv_z) in zip(range(c0, c0 + group), stats):
                cols = slice(c * LANES, (c + 1) * LANES)
                dst = (h, c // per, slice(None), slice((c % per) * LANES, (c % per + 1) * LANES))
                s1c = s1[:, cols]
                s2c = s2[:, cols]
                thr = tau - s1c
                take = jnp.zeros_like(s1c)
                rank = jnp.zeros_like(s2c)
                for j in range(kk):
                    tj = top2[j:j + 1]
                    take = jnp.where(tj >= thr, j + 1.0, take)
                    rank = jnp.where(tj > s2c, j + 1.0, rank)
                take_ref[dst] = take
                rank_ref[dst] = rank.astype(BF16)
                e1_ref[dst] = jnp.exp(s1c - m1)
                e2_ref[dst] = (jnp.exp(s2c - top2[0:1]) * (0.5 * inv_z)).astype(BF16)
        return 0

    lax.fori_loop(0, PEER_HEADS, head, 0)


def _peer_select_call(n2, h, p, wq, keys, wple, wpg, tt):
    t, d = n2.shape
    pd = p.shape[1]
    dq = wq.shape[1]
    nh, nk = PEER_HEADS, PEER_KEYS
    tspec = pl.BlockSpec((nh, tt // PEER_CHUNK, nk, PEER_CHUNK), lambda i: (0, i, 0, 0))
    oshape = (nh, t // PEER_CHUNK, nk, PEER_CHUNK)
    tshapes = [jax.ShapeDtypeStruct(oshape, dt) for dt in (F32, BF16, F32, BF16)]
    return pl.pallas_call(
        _peer_select_kernel,
        grid=(t // tt,),
        in_specs=[_row_spec(tt, d), _row_spec(tt, d), _row_spec(tt, pd),
                  _const_spec((d, dq)), _const_spec(keys.shape),
                  _const_spec((pd, d)), _const_spec((d, d))],
        out_specs=[tspec] * 4 + [_row_spec(tt, d)],
        out_shape=tshapes + [jax.ShapeDtypeStruct((t, d), F32)],
        scratch_shapes=[pltpu.VMEM((2 * nh, tt, PEER_HALF), BF16)],
        compiler_params=_cparams("parallel"),
        name="peer_select",
    )(n2, h, p, wq, keys, wple, wpg)


def _peer_dense_kernel(n2_ref, u_ref, vt_ref, take_ref, e1_ref, rank_ref, e2_ref, hp_ref,
                       out_ref, acc_ref, act0_ref, act1_ref, ga0_ref, ga1_ref, *, nj):
    f = pl.program_id(0)
    nt = (((1,), (1,)), ((), ()))
    nc, te, cw = act0_ref.shape
    nk = PEER_KEYS
    rb = 2 * SUBLANES
    j3 = jnp.maximum(f - 2, 0) % nj

    @pl.when(f == 0)
    def _():
        act1_ref[...] = jnp.zeros_like(act1_ref)
        ga0_ref[...] = jnp.zeros_like(ga0_ref)

    @pl.when(j3 == 0)
    def _():
        acc_ref[...] = jnp.zeros_like(acc_ref)

    def stages(act_new, act_cur, ga_cur, ga_old):
        for c in range(nc):
            tok = slice(c * cw, (c + 1) * cw)
            act_new[c] = lax.dot_general(u_ref[...], n2_ref[tok, :], nt,
                                         preferred_element_type=F32)
            for i1 in range(te // nk):
                def row16(ref, h):
                    return jnp.broadcast_to(ref[h, c, i1:i1 + 1, :], (rb, cw)).astype(BF16)

                take = [row16(take_ref, h) for h in range(PEER_HEADS)]
                e1 = [row16(e1_ref, h) for h in range(PEER_HEADS)]
                zero = jnp.zeros((rb, cw), BF16)
                for r in range(nk // rb):
                    rs = slice(r * rb, (r + 1) * rb)
                    g = None
                    for h in range(PEER_HEADS):
                        keep = rank_ref[h, c, rs, :] < take[h]
                        term = jnp.where(keep, e2_ref[h, c, rs, :], zero) * e1[h]
                        g = term if g is None else g + term
                    rows = slice(i1 * nk + r * rb, i1 * nk + (r + 1) * rb)
                    ga_cur[c, rows, :] = g * _gelu_x2(act_cur[c, rows, :].astype(BF16))
            acc_ref[c] += _dot(vt_ref[...], ga_old[c])

    @pl.when(f % 2 == 0)
    def _():
        stages(act0_ref, act1_ref, ga1_ref, ga0_ref)

    @pl.when(f % 2 == 1)
    def _():
        stages(act1_ref, act0_ref, ga0_ref, ga1_ref)

    @pl.when(jnp.logical_and(f >= 2, j3 == nj - 1))
    def _():
        for c in range(nc):
            out_ref[c * cw:(c + 1) * cw, :] = hp_ref[c * cw:(c + 1) * cw, :] + acc_ref[c].T


def _peer_dense_call(n2, u, vt, take, e1, rank, e2, hp, tt, te):
    t, d = n2.shape
    ne = u.shape[0]
    nh, nk = PEER_HEADS, PEER_KEYS
    r1 = te // nk
    cw = PEER_CHUNK
    nc = tt // cw
    ni, nj = t // tt, ne // te
    last = ni * nj - 1

    def pair(f, lag):
        a = jnp.clip(f - lag, 0, last)
        return a // nj, a % nj

    score_blk = (nh, nc, nk, cw)
    row_blk = (nh, nc, r1, cw)
    return pl.pallas_call(
        functools.partial(_peer_dense_kernel, nj=nj),
        grid=(ni * nj + 2,),
        in_specs=[pl.BlockSpec((tt, d), lambda f: (pair(f, 0)[0], 0)),
                  pl.BlockSpec((te, d), lambda f: (pair(f, 0)[1], 0)),
                  pl.BlockSpec((d, te), lambda f: (0, pair(f, 2)[1])),
                  pl.BlockSpec(row_blk, lambda f: (0, pair(f, 1)[0], pair(f, 1)[1], 0)),
                  pl.BlockSpec(row_blk, lambda f: (0, pair(f, 1)[0], pair(f, 1)[1], 0)),
                  pl.BlockSpec(score_blk, lambda f: (0, pair(f, 1)[0], 0, 0)),
                  pl.BlockSpec(score_blk, lambda f: (0, pair(f, 1)[0], 0, 0)),
                  pl.BlockSpec((tt, d), lambda f: (pair(f, 2)[0], 0))],
        out_specs=pl.BlockSpec((tt, d), lambda f: (pair(f, 2)[0], 0)),
        out_shape=jax.ShapeDtypeStruct((t, d), F32),
        scratch_shapes=[pltpu.VMEM((nc, d, cw), F32)]
        + [pltpu.VMEM((nc, te, cw), F32)] * 2 + [pltpu.VMEM((nc, te, cw), BF16)] * 2,
        compiler_params=_cparams("arbitrary"),
        name="peer_dense",
    )(n2, u, vt, take, e1, rank, e2, hp)


def _final_norm_kernel(h_ref, g_ref, o_ref):
    o_ref[0] = _rms(h_ref[...], g_ref[...])


def _final_norm_call(h2d, g, batch, seq, ts):
    d = g.shape[-1]
    return pl.pallas_call(
        _final_norm_kernel,
        grid=(seq // ts, batch),
        in_specs=[pl.BlockSpec((ts, d), lambda s, b: (s, b)), _const_spec((1, d))],
        out_specs=pl.BlockSpec((1, ts, d), lambda s, b: (b, s, 0)),
        out_shape=jax.ShapeDtypeStruct((batch, seq, d), F32),
        compiler_params=_cparams("parallel", "parallel"),
        name="final_norm",
    )(h2d, g.reshape(1, d))


def _tile(total, want):
    want = min(want, total)
    assert total % want == 0, (total, want)
    return want


def kernel(x, p, norm_mix, norm_ffn, norm_final, w_in, rw_w0, rw_w2, rw_a0, rw_a2, rw_g2, rw_kk, rw_ka, rw_rk, rw_ln_g, rw_ln_b, rw_mv_w1, rw_mv_w2, rw_mv_v0, gm_ln_g, gm_ln_b, gm_ws, gm_bs, w_proj_a, w_proj_b, w_out, peer_wq, peer_keys, peer_u, peer_v, ple_w, ple_gate):
    batch, seq, d = x.shape
    depth = w_in.shape[0]
    t = batch * seq
    w = RW_HEADS * RW_HEAD_DIM
    assert batch * RW_HEADS == LANES and batch == SUBLANES and d == w
    assert seq % GM_CHUNK == 0

    tm = _tile(t, 512)
    tt = _tile(t, 512)
    te = 2048
    steps = _tile(seq, 64)

    gw = gm_ln_g.shape[1]
    sizes = (w, w, w, rw_w2.shape[1], rw_a2.shape[1], rw_g2.shape[1], gw, gw, d, d)
    offs = [0]
    for sz in sizes:
        offs.append(offs[-1] + sz)

    def col(wi, idx):
        return wi[:, offs[idx]:offs[idx + 1]].astype(BF16)

    h = x.transpose(1, 0, 2).reshape(t, d)
    pt = p.transpose(0, 2, 1, 3).reshape(depth, t, p.shape[-1]).astype(BF16)
    row = lambda a: a.reshape(1, -1).astype(F32)
    expand = jnp.repeat(jnp.eye(GM_CHUNK, dtype=F32), batch, axis=0)
    same_batch = jnp.tile(jnp.eye(batch, dtype=F32), (GM_CHUNK, GM_CHUNK))
    tril = jnp.tril(jnp.ones((GM_CHUNK, GM_CHUNK), F32))

    v_first = None
    for i in range(depth):
        wi = w_in[i]
        nx = _norm_shift(h, norm_mix[i], batch, tm)

        r, k, kkraw, ar = _rk_call(nx, col(wi, 0), col(wi, 1), col(wi, 4), rw_a2[i].astype(BF16),
                                   row(rw_a0[i]), row(rw_kk[i]), row(rw_ka[i]), tm)
        mv = None
        if i > 0:
            mv = (rw_mv_w1[i - 1].astype(BF16), rw_mv_w2[i - 1].astype(BF16), row(rw_mv_v0[i - 1]))
        v, dec = _vw_call(nx, col(wi, 2), col(wi, 3), rw_w2[i].astype(BF16), row(rw_w0[i]), mv,
                          v_first, tm)
        if i == 0:
            v_first = v

        y = _wkv_call(r, dec, k, v, kkraw, ar,
                      _head_param_lanes(rw_rk[i], batch),
                      _head_param_lanes(rw_ln_g[i].reshape(RW_HEADS, RW_HEAD_DIM), batch),
                      _head_param_lanes(rw_ln_b[i].reshape(RW_HEADS, RW_HEAD_DIM), batch),
                      batch, steps)

        ws_k = (jnp.einsum("ri,gij,cj->grc", expand, gm_ws[i] * tril, expand)
                * same_batch).astype(BF16)
        bias = jnp.repeat(gm_bs[i].T, batch, axis=0).astype(F32)
        yb = _gmlp_call(nx, col(wi, 6), col(wi, 7), row(gm_ln_g[i]), row(gm_ln_b[i]), ws_k, bias)

        h, n2 = _merge_call(nx, y, yb, h, col(wi, 5), rw_g2[i].astype(BF16), col(wi, 8), col(wi, 9),
                            w_proj_a[i].astype(BF16), w_proj_b[i].astype(BF16),
                            w_out[i].astype(BF16), row(norm_ffn[i]), tm)

        take, rank, e1, e2, hp = _peer_select_call(
            n2, h, pt[i], peer_wq[i].astype(BF16), peer_keys[i].astype(BF16),
            ple_w[i].astype(BF16), ple_gate[i].astype(BF16), tt)
        h = _peer_dense_call(n2, peer_u[i].astype(BF16), peer_v[i].T.astype(BF16),
                             take, e1, rank, e2, hp, tt, te)

    return _final_norm_call(h.reshape(seq, batch * d), norm_final, batch, seq, _tile(seq, 512))
```

```python
import functools
import math

import jax
import jax.numpy as jnp
from jax import lax
from jax.experimental import pallas as pl
from jax.experimental.pallas import tpu as pltpu

F32 = jnp.float32
BF16 = jnp.bfloat16

RW_HEADS = 16
RW_HEAD_DIM = 64
RW_LN_EPS = 64e-5
GM_GROUPS = 8
GM_CHUNK = 128
GM_LN_EPS = 1e-5
PEER_HEADS = 8
PEER_KEYS = 128
PEER_HALF = 128
PEER_TOPK = 16
PEER_CHUNK = 256
RMS_EPS = 1e-6

LANES = 128
SUBLANES = 8
VMEM_LIMIT = 56 * 1024 * 1024


def _cparams(*sem):
    return pltpu.CompilerParams(dimension_semantics=tuple(sem), vmem_limit_bytes=VMEM_LIMIT)


def _const_spec(shape):
    nd = len(shape)
    return pl.BlockSpec(shape, lambda *_: (0,) * nd, pipeline_mode=pl.Buffered(1))


def _row_spec(rows, cols):
    return pl.BlockSpec((rows, cols), lambda i: (i, 0))


def _gelu(x):
    c = math.sqrt(2.0 / math.pi)
    return 0.5 * x * (1.0 + jnp.tanh(c * (x + 0.044715 * (x * x * x))))


def _gelu_x2(x):
    c = math.sqrt(2.0 / math.pi)
    return x * (1.0 + jnp.tanh(x * (c + (0.044715 * c) * (x * x))))


def _sigmoid(x):
    return 1.0 / (1.0 + jnp.exp(-x))


def _dot(a, b):
    return jnp.dot(a, b, preferred_element_type=F32)


def _rms(x, g):
    return x * lax.rsqrt(jnp.mean(x * x, axis=-1, keepdims=True) + RMS_EPS) * g


def _norm_shift_kernel(h_ref, hprev_ref, g_ref, nx_ref, *, batch):
    i = pl.program_id(0)
    g = g_ref[...]
    n = _rms(h_ref[...], g)
    prev = _rms(hprev_ref[...], g) * (i > 0).astype(F32)
    d = n.shape[1]
    shifted = jnp.concatenate([prev, n[:-batch]], axis=0)
    nx_ref[:, :d] = n.astype(BF16)
    nx_ref[:, d:] = shifted.astype(BF16)


def _norm_shift(h, g, batch, tm):
    t, d = h.shape
    per = tm // batch
    return pl.pallas_call(
        functools.partial(_norm_shift_kernel, batch=batch),
        grid=(t // tm,),
        in_specs=[_row_spec(tm, d),
                  pl.BlockSpec((batch, d), lambda i: (jnp.maximum(i * per - 1, 0), 0)),
                  _const_spec((1, d))],
        out_specs=_row_spec(tm, 2 * d),
        out_shape=jax.ShapeDtypeStruct((t, 2 * d), BF16),
        compiler_params=_cparams("parallel"),
        name="norm_shift",
    )(h, h, g.reshape(1, d))


def _rk_kernel(nx_ref, wr_ref, wk_ref, wa_ref, a2_ref, a0_ref, kk_ref, ka_ref,
               r_ref, k_ref, kkraw_ref, ar_ref):
    nx = nx_ref[...]
    r_ref[...] = _dot(nx, wr_ref[...])
    zk = _dot(nx, wk_ref[...])
    za = _dot(nx, wa_ref[...])
    a_rate = _sigmoid(a0_ref[...] + _dot(za.astype(BF16), a2_ref[...]))
    ar_ref[...] = a_rate
    kkraw_ref[...] = zk * kk_ref[...]
    k_ref[...] = zk * (1.0 + (a_rate - 1.0) * ka_ref[...])


def _rk_call(nx, wr, wk, wa, a2, a0, kk, ka, tm):
    t, d2 = nx.shape
    w = wr.shape[1]
    la = wa.shape[1]
    out = jax.ShapeDtypeStruct((t, w), F32)
    return pl.pallas_call(
        _rk_kernel,
        grid=(t // tm,),
        in_specs=[_row_spec(tm, d2), _const_spec((d2, w)), _const_spec((d2, w)),
                  _const_spec((d2, la)), _const_spec((la, w)),
                  _const_spec((1, w)), _const_spec((1, w)), _const_spec((1, w))],
        out_specs=[_row_spec(tm, w)] * 4,
        out_shape=[out] * 4,
        compiler_params=_cparams("parallel"),
        name="rwkv_rk",
    )(nx, wr, wk, wa, a2, a0, kk, ka)


def _decay_from(nx, ww_ref, w2_ref, w0_ref):
    zw = _dot(nx, ww_ref[...])
    x = -(w0_ref[...] + _dot(jnp.tanh(zw).astype(BF16), w2_ref[...]))
    softplus = jnp.maximum(x, 0.0) + jnp.log(1.0 + jnp.exp(-jnp.abs(x)))
    return jnp.exp(-jnp.exp(-softplus - 0.5))


def _vw0_kernel(nx_ref, wv_ref, ww_ref, w2_ref, w0_ref, v_ref, dec_ref):
    nx = nx_ref[...]
    v_ref[...] = _dot(nx, wv_ref[...])
    dec_ref[...] = _decay_from(nx, ww_ref, w2_ref, w0_ref)


def _vw_kernel(nx_ref, wv_ref, ww_ref, w2_ref, w0_ref, m1_ref, m2_ref, v0_ref, vfirst_ref,
               v_ref, dec_ref):
    nx = nx_ref[...]
    zv = _dot(nx, wv_ref[...])
    lo = _dot(nx, m1_ref[...])
    vmix = _sigmoid(v0_ref[...] + _dot(lo.astype(BF16), m2_ref[...]))
    v_ref[...] = zv + (vfirst_ref[...] - zv) * vmix
    dec_ref[...] = _decay_from(nx, ww_ref, w2_ref, w0_ref)


def _vw_call(nx, wv, ww, w2, w0, mv, vfirst, tm):
    t, d2 = nx.shape
    w = wv.shape[1]
    lw = ww.shape[1]
    out = jax.ShapeDtypeStruct((t, w), F32)
    in_specs = [_row_spec(tm, d2), _const_spec((d2, w)), _const_spec((d2, lw)),
                _const_spec((lw, w)), _const_spec((1, w))]
    args = [nx, wv, ww, w2, w0]
    if mv is None:
        body = _vw0_kernel
    else:
        m1, m2, v0 = mv
        lm = m1.shape[1]
        body = _vw_kernel
        in_specs += [_const_spec((d2, lm)), _const_spec((lm, w)), _const_spec((1, w)),
                     _row_spec(tm, w)]
        args += [m1, m2, v0, vfirst]
    return pl.pallas_call(
        body,
        grid=(t // tm,),
        in_specs=in_specs,
        out_specs=[_row_spec(tm, w)] * 2,
        out_shape=[out] * 2,
        compiler_params=_cparams("parallel"),
        name="rwkv_vw",
    )(*args)


def _to_lanes(blk):
    r = jnp.concatenate([blk[s * 8:(s + 1) * 8, j * LANES:(j + 1) * LANES]
                         for s in range(2) for j in range(8)], axis=0)
    tr = r.T
    lo, hi = tr[0:64], tr[64:128]
    lane = lax.broadcasted_iota(jnp.int32, (64, LANES), 1)
    first = lane < 64
    return (jnp.where(first, lo, pltpu.roll(hi, 64, 1)),
            jnp.where(first, pltpu.roll(lo, 64, 1), hi))


def _from_lanes(o0, o1):
    lane = lax.broadcasted_iota(jnp.int32, (64, LANES), 1)
    first = lane < 64
    lo = jnp.where(first, o0, pltpu.roll(o1, 64, 1))
    hi = jnp.where(first, pltpu.roll(o0, 64, 1), o1)
    r = jnp.concatenate([lo, hi], axis=0).T
    rows = [jnp.concatenate([r[s * 64 + j * 8:s * 64 + j * 8 + 8] for j in range(8)], axis=1)
            for s in range(2)]
    return jnp.concatenate(rows, axis=0)


_WR, _WW, _WK, _WV, _WA, _WB = range(6)


def _wkv_kernel(r_ref, w_ref, k_ref, v_ref, kk_ref, ar_ref, rk_ref, lng_ref, lnb_ref,
                y_ref, st_ref, buf_a, buf_b, y_a, y_b, *, steps):
    n = RW_HEAD_DIM
    pairs = steps // 2

    @pl.when(pl.program_id(0) == 0)
    def _():
        st_ref[...] = jnp.zeros_like(st_ref)

    def load_pair(sp, buf):
        rows = pl.ds(pl.multiple_of(sp * 16, 16), 16)
        for idx, src in ((_WR, r_ref), (_WW, w_ref), (_WK, k_ref), (_WV, v_ref)):
            o0, o1 = _to_lanes(src[rows, :])
            buf[idx, 0] = o0
            buf[idx, 1] = o1
        kk0, kk1 = _to_lanes(kk_ref[rows, :])
        ar0, ar1 = _to_lanes(ar_ref[rows, :])
        for off, kk, ar in ((0, kk0, ar0), (1, kk1, ar1)):
            ss = jnp.sum(kk * kk, axis=0, keepdims=True)
            kkn = kk * lax.rsqrt(jnp.maximum(ss, 1e-24))
            buf[_WA, off] = -kkn
            buf[_WB, off] = kkn * ar

    def step(buf, off, nbuf, noff, ybuf, sa):
        nxt = []
        hv = n // 4
        for half in range(n // hv):
            vs = slice(half * hv, (half + 1) * hv)
            sa_h = sa[vs]
            vv = buf[_WV, off, vs, :]
            y = jnp.zeros((hv, LANES), F32)
            san = jnp.zeros((hv, LANES), F32)
            for k in range(n):
                row = slice(k, k + 1)
                sk = (st_ref[k, vs, :] * buf[_WW, off, row, :] + sa_h * buf[_WB, off, row, :]
                      + vv * buf[_WK, off, row, :])
                st_ref[k, vs, :] = sk
                y = y + sk * buf[_WR, off, row, :]
                san = san + sk * nbuf[_WA, noff, row, :]
            ybuf[off, vs, :] = y
            nxt.append(san)
        return jnp.concatenate(nxt, axis=0)

    rk = rk_ref[...]
    lng = lng_ref[...]
    lnb = lnb_ref[...]

    def finish(buf, off, ybuf):
        y = ybuf[off]
        mu = jnp.mean(y, axis=0, keepdims=True)
        yc = y - mu
        var = jnp.mean(yc * yc, axis=0, keepdims=True)
        yn = yc * lax.rsqrt(var + RW_LN_EPS) * lng + lnb
        bonus = jnp.sum(buf[_WR, off] * buf[_WK, off] * rk, axis=0, keepdims=True)
        return yn + bonus * buf[_WV, off]

    def store_pair(sp, buf, ybuf):
        rows = pl.ds(pl.multiple_of(sp * 16, 16), 16)
        y_ref[rows, :] = _from_lanes(finish(buf, 0, ybuf), finish(buf, 1, ybuf))

    load_pair(0, buf_a)
    sa0 = jnp.zeros((n, LANES), F32)
    for k in range(n):
        sa0 = sa0 + st_ref[k] * buf_a[_WA, 0, k:k + 1, :]

    def two_pairs(q, sa):
        sa = step(buf_a, 0, buf_a, 1, y_a, sa)
        load_pair(2 * q + 1, buf_b)
        sa = step(buf_a, 1, buf_b, 0, y_a, sa)
        sa = step(buf_b, 0, buf_b, 1, y_b, sa)
        store_pair(2 * q, buf_a, y_a)
        load_pair(jnp.minimum(2 * q + 2, pairs - 1), buf_a)
        sa = step(buf_b, 1, buf_a, 0, y_b, sa)
        store_pair(2 * q + 1, buf_b, y_b)
        return sa

    lax.fori_loop(0, pairs // 2, two_pairs, sa0)


def _wkv_call(r, w, k, v, kkraw, ar, rk_l, lng_l, lnb_l, batch, steps):
    t, width = r.shape
    rows = steps * batch
    n = RW_HEAD_DIM
    assert steps % 4 == 0
    pair_buf = pltpu.VMEM((6, 2, n, LANES), F32)
    pair_out = pltpu.VMEM((2, n, LANES), F32)
    return pl.pallas_call(
        functools.partial(_wkv_kernel, steps=steps),
        grid=(t // rows,),
        in_specs=[_row_spec(rows, width)] * 6 + [_const_spec((n, LANES))] * 3,
        out_specs=_row_spec(rows, width),
        out_shape=jax.ShapeDtypeStruct((t, width), F32),
        scratch_shapes=[pltpu.VMEM((n, n, LANES), F32), pair_buf, pair_buf, pair_out, pair_out],
        compiler_params=_cparams("arbitrary"),
        name="wkv7_scan",
    )(r, w, k, v, kkraw, ar, rk_l, lng_l, lnb_l)


def _head_param_lanes(pv, batch):
    a = pv.reshape(RW_HEADS // 2, 2, RW_HEAD_DIM).transpose(2, 1, 0)
    a = jnp.broadcast_to(a[..., None], a.shape + (batch,))
    return a.reshape(RW_HEAD_DIM, LANES).astype(F32)


def _gmlp_kernel(nx_ref, wu_ref, wg_ref, lng_ref, lnb_ref, ws_ref, bias_ref, yb_ref):
    nx = nx_ref[...]
    u = _gelu(_dot(nx, wu_ref[...]))
    gv = _gelu(_dot(nx, wg_ref[...]))
    mu = jnp.mean(gv, axis=-1, keepdims=True)
    gc = gv - mu
    var = jnp.mean(gc * gc, axis=-1, keepdims=True)
    v = (gc * lax.rsqrt(var + GM_LN_EPS) * lng_ref[...] + lnb_ref[...]).astype(BF16)
    gd = v.shape[1] // GM_GROUPS
    for g in range(GM_GROUPS):
        cols = slice(g * gd, (g + 1) * gd)
        s = _dot(ws_ref[g], v[:, cols]) + bias_ref[:, g:g + 1]
        yb_ref[:, cols] = (u[:, cols] * s).astype(BF16)


def _gmlp_call(nx, wu, wg, lng, lnb, ws_k, bias):
    t, d2 = nx.shape
    w = wu.shape[1]
    rows = ws_k.shape[1]
    return pl.pallas_call(
        _gmlp_kernel,
        grid=(t // rows,),
        in_specs=[_row_spec(rows, d2), _const_spec((d2, w)), _const_spec((d2, w)),
                  _const_spec((1, w)), _const_spec((1, w)),
                  _const_spec(ws_k.shape), _const_spec((rows, GM_GROUPS))],
        out_specs=_row_spec(rows, w),
        out_shape=jax.ShapeDtypeStruct((t, w), BF16),
        compiler_params=_cparams("parallel"),
        name="gmlp",
    )(nx, wu, wg, lng, lnb, ws_k, bias)


def _merge_kernel(nx_ref, y_ref, yb_ref, h_ref, wg_ref, g2_ref, wga_ref, wgb_ref,
                  wa_ref, wb_ref, wo_ref, nf_ref, hn_ref, n2_ref):
    nx = nx_ref[...]
    gate = _dot(_sigmoid(_dot(nx, wg_ref[...])).astype(BF16), g2_ref[...])
    ya = (y_ref[...] * gate).astype(BF16)
    pa = _dot(ya, wa_ref[...])
    pb = _dot(yb_ref[...], wb_ref[...])
    merged = _sigmoid(_dot(nx, wga_ref[...])) * pa + _sigmoid(_dot(nx, wgb_ref[...])) * pb
    hn = h_ref[...] + _dot(merged.astype(BF16), wo_ref[...])
    hn_ref[...] = hn
    n2_ref[...] = _rms(hn, nf_ref[...]).astype(BF16)


def _merge_call(nx, y, yb, h, wg, g2, wga, wgb, wa, wb, wo, nf, tm):
    t, d2 = nx.shape
    d = h.shape[1]
    w = y.shape[1]
    lg = wg.shape[1]
    return pl.pallas_call(
        _merge_kernel,
        grid=(t // tm,),
        in_specs=[_row_spec(tm, d2), _row_spec(tm, w), _row_spec(tm, w), _row_spec(tm, d),
                  _const_spec((d2, lg)), _const_spec((lg, w)),
                  _const_spec((d2, d)), _const_spec((d2, d)),
                  _const_spec((w, d)), _const_spec((w, d)), _const_spec((d, d)),
                  _const_spec((1, d))],
        out_specs=[_row_spec(tm, d), _row_spec(tm, d)],
        out_shape=[jax.ShapeDtypeStruct((t, d), F32), jax.ShapeDtypeStruct((t, d), BF16)],
        compiler_params=_cparams("parallel"),
        name="merge",
    )(nx, y, yb, h, wg, g2, wga, wgb, wa, wb, wo, nf)


def _bitonic_exchanges(n):
    out = []
    k = 2
    while k <= n:
        j = k // 2
        while j >= 1:
            for i in range(n):
                l = i ^ j
                if l > i:
                    out.append((i, l, (i & k) == 0))
            j //= 2
        k *= 2
    return out


def _top_values(s, count, rows):
    slabs = [s[i:i + SUBLANES] for i in range(0, s.shape[0], SUBLANES)]
    wires = 1
    while wires < len(slabs):
        wires *= 2
    slabs += [None] * (wires - len(slabs))
    for i, j, ascending in _bitonic_exchanges(wires):
        a, b = slabs[i], slabs[j]
        if a is None and b is None:
            continue
        if a is None or b is None:
            hi, lo = (a if b is None else b), None
        else:
            hi, lo = jnp.maximum(a, b), jnp.minimum(a, b)
        slabs[i], slabs[j] = (lo, hi) if ascending else (hi, lo)
    lists = slabs[::-1]
    neg = jnp.full((SUBLANES, s.shape[1]), -jnp.inf, s.dtype)
    lists = [neg if x is None else x for x in lists]
    tops = []
    for it in range(count):
        m = jnp.max(lists[0], axis=0, keepdims=True)
        tops.append(m)
        hit = lists[0] == m
        depth = min(count - it - 1, len(lists))
        for j in range(depth):
            below = lists[j + 1] if j + 1 < len(lists) else neg
            lists[j] = jnp.where(hit, below, lists[j])
    tops += [jnp.full_like(tops[0], -jnp.inf)] * (rows - count)
    return jnp.concatenate(tops, axis=0)


def _peer_select_kernel(n2_ref, h_ref, p_ref, wq_ref, keys_ref, wple_ref, wpg_ref,
                        take_ref, rank_ref, e1_ref, e2_ref, hp_ref, q_scr):
    n2 = n2_ref[...]
    ple = _dot(p_ref[...], wple_ref[...]) * _sigmoid(_dot(n2, wpg_ref[...]))
    hp_ref[...] = h_ref[...] + ple
    q = _dot(n2, wq_ref[...]).astype(BF16)
    for hp in range(2 * PEER_HEADS):
        q_scr[hp] = q[:, hp * PEER_HALF:(hp + 1) * PEER_HALF]
    tt = n2.shape[0]
    kk = PEER_TOPK
    nt = (((1,), (1,)), ((), ()))

    def head(h, _):
        s1 = lax.dot_general(keys_ref[h, 0], q_scr[2 * h], nt, preferred_element_type=F32)
        s2 = lax.dot_general(keys_ref[h, 1], q_scr[2 * h + 1], nt, preferred_element_type=F32)
        per = PEER_CHUNK // LANES
        pad = 3 * SUBLANES
        for c in range(tt // LANES):
            cols = slice(c * LANES, (c + 1) * LANES)
            dst = (h, c // per, slice(None), slice((c % per) * LANES, (c % per + 1) * LANES))
            s1c = s1[:, cols]
            s2c = s2[:, cols]
            t1 = _top_values(s1c, kk + 1, pad)
            t2 = _top_values(s2c, kk + 1, pad)
            cand = [t1[0:1] + t2]
            cand += [t1[i:i + 1] + t2[0:SUBLANES] for i in range(1, SUBLANES)]
            cand += [t1[SUBLANES:pad] + t2[0:1]]
            best = _top_values(jnp.concatenate(cand, axis=0), kk + 1, kk + 1)
            top = best[0:kk]
            inv_z = 1.0 / jnp.sum(jnp.exp(top - top[0:1]), axis=0, keepdims=True)
            tau = 0.5 * (best[kk - 1:kk] + best[kk:kk + 1])
            thr = tau - s1c
            take = jnp.zeros_like(s1c)
            rank = jnp.zeros_like(s2c)
            for j in range(kk):
                tj = t2[j:j + 1]
                take = jnp.where(tj >= thr, j + 1.0, take)
                rank = jnp.where(tj > s2c, j + 1.0, rank)
            take_ref[dst] = take
            rank_ref[dst] = rank.astype(BF16)
            e1_ref[dst] = jnp.exp(s1c - t1[0:1])
            e2_ref[dst] = (jnp.exp(s2c - t2[0:1]) * (0.5 * inv_z)).astype(BF16)
        return 0

    lax.fori_loop(0, PEER_HEADS, head, 0)


def _peer_select_call(n2, h, p, wq, keys, wple, wpg, tt):
    t, d = n2.shape
    pd = p.shape[1]
    dq = wq.shape[1]
    nh, nk = PEER_HEADS, PEER_KEYS
    tspec = pl.BlockSpec((nh, tt // PEER_CHUNK, nk, PEER_CHUNK), lambda i: (0, i, 0, 0))
    oshape = (nh, t // PEER_CHUNK, nk, PEER_CHUNK)
    tshapes = [jax.ShapeDtypeStruct(oshape, dt) for dt in (F32, BF16, F32, BF16)]
    return pl.pallas_call(
        _peer_select_kernel,
        grid=(t // tt,),
        in_specs=[_row_spec(tt, d), _row_spec(tt, d), _row_spec(tt, pd),
                  _const_spec((d, dq)), _const_spec(keys.shape),
                  _const_spec((pd, d)), _const_spec((d, d))],
        out_specs=[tspec] * 4 + [_row_spec(tt, d)],
        out_shape=tshapes + [jax.ShapeDtypeStruct((t, d), F32)],
        scratch_shapes=[pltpu.VMEM((2 * nh, tt, PEER_HALF), BF16)],
        compiler_params=_cparams("parallel"),
        name="peer_select",
    )(n2, h, p, wq, keys, wple, wpg)


def _peer_dense_kernel(n2_ref, u_ref, vt_ref, take_ref, e1_ref, rank_ref, e2_ref, hp_ref,
                       out_ref, acc_ref, act0_ref, act1_ref, ga0_ref, ga1_ref, *, nj):
    f = pl.program_id(0)
    nt = (((1,), (1,)), ((), ()))
    nc, te, cw = act0_ref.shape
    nk = PEER_KEYS
    rb = 2 * SUBLANES
    j3 = jnp.maximum(f - 2, 0) % nj

    @pl.when(f == 0)
    def _():
        act1_ref[...] = jnp.zeros_like(act1_ref)
        ga0_ref[...] = jnp.zeros_like(ga0_ref)

    @pl.when(j3 == 0)
    def _():
        acc_ref[...] = jnp.zeros_like(acc_ref)

    def stages(act_new, act_cur, ga_cur, ga_old):
        for c in range(nc):
            tok = slice(c * cw, (c + 1) * cw)
            act_new[c] = lax.dot_general(u_ref[...], n2_ref[tok, :], nt,
                                         preferred_element_type=F32)
            for i1 in range(te // nk):
                def row16(ref, h):
                    return jnp.broadcast_to(ref[h, c, i1:i1 + 1, :], (rb, cw)).astype(BF16)

                take = [row16(take_ref, h) for h in range(PEER_HEADS)]
                e1 = [row16(e1_ref, h) for h in range(PEER_HEADS)]
                zero = jnp.zeros((rb, cw), BF16)
                for r in range(nk // rb):
                    rs = slice(r * rb, (r + 1) * rb)
                    g = None
                    for h in range(PEER_HEADS):
                        keep = rank_ref[h, c, rs, :] < take[h]
                        term = jnp.where(keep, e2_ref[h, c, rs, :], zero) * e1[h]
                        g = term if g is None else g + term
                    rows = slice(i1 * nk + r * rb, i1 * nk + (r + 1) * rb)
                    ga_cur[c, rows, :] = g * _gelu_x2(act_cur[c, rows, :].astype(BF16))
            acc_ref[c] += _dot(vt_ref[...], ga_old[c])

    @pl.when(f % 2 == 0)
    def _():
        stages(act0_ref, act1_ref, ga1_ref, ga0_ref)

    @pl.when(f % 2 == 1)
    def _():
        stages(act1_ref, act0_ref, ga0_ref, ga1_ref)

    @pl.when(jnp.logical_and(f >= 2, j3 == nj - 1))
    def _():
        for c in range(nc):
            out_ref[c * cw:(c + 1) * cw, :] = hp_ref[c * cw:(c + 1) * cw, :] + acc_ref[c].T


def _peer_dense_call(n2, u, vt, take, e1, rank, e2, hp, tt, te):
    t, d = n2.shape
    ne = u.shape[0]
    nh, nk = PEER_HEADS, PEER_KEYS
    r1 = te // nk
    cw = PEER_CHUNK
    nc = tt // cw
    ni, nj = t // tt, ne // te
    last = ni * nj - 1

    def pair(f, lag):
        a = jnp.clip(f - lag, 0, last)
        return a // nj, a % nj

    score_blk = (nh, nc, nk, cw)
    row_blk = (nh, nc, r1, cw)
    return pl.pallas_call(
        functools.partial(_peer_dense_kernel, nj=nj),
        grid=(ni * nj + 2,),
        in_specs=[pl.BlockSpec((tt, d), lambda f: (pair(f, 0)[0], 0)),
                  pl.BlockSpec((te, d), lambda f: (pair(f, 0)[1], 0)),
                  pl.BlockSpec((d, te), lambda f: (0, pair(f, 2)[1])),
                  pl.BlockSpec(row_blk, lambda f: (0, pair(f, 1)[0], pair(f, 1)[1], 0)),
                  pl.BlockSpec(row_blk, lambda f: (0, pair(f, 1)[0], pair(f, 1)[1], 0)),
                  pl.BlockSpec(score_blk, lambda f: (0, pair(f, 1)[0], 0, 0)),
                  pl.BlockSpec(score_blk, lambda f: (0, pair(f, 1)[0], 0, 0)),
                  pl.BlockSpec((tt, d), lambda f: (pair(f, 2)[0], 0))],
        out_specs=pl.BlockSpec((tt, d), lambda f: (pair(f, 2)[0], 0)),
        out_shape=jax.ShapeDtypeStruct((t, d), F32),
        scratch_shapes=[pltpu.VMEM((nc, d, cw), F32)]
        + [pltpu.VMEM((nc, te, cw), F32)] * 2 + [pltpu.VMEM((nc, te, cw), BF16)] * 2,
        compiler_params=_cparams("arbitrary"),
        name="peer_dense",
    )(n2, u, vt, take, e1, rank, e2, hp)


def _final_norm_kernel(h_ref, g_ref, o_ref):
    o_ref[0] = _rms(h_ref[...], g_ref[...])


def _final_norm_call(h2d, g, batch, seq, ts):
    d = g.shape[-1]
    return pl.pallas_call(
        _final_norm_kernel,
        grid=(seq // ts, batch),
        in_specs=[pl.BlockSpec((ts, d), lambda s, b: (s, b)), _const_spec((1, d))],
        out_specs=pl.BlockSpec((1, ts, d), lambda s, b: (b, s, 0)),
        out_shape=jax.ShapeDtypeStruct((batch, seq, d), F32),
        compiler_params=_cparams("parallel", "parallel"),
        name="final_norm",
    )(h2d, g.reshape(1, d))


def _tile(total, want):
    want = min(want, total)
    assert total % want == 0, (total, want)
    return want


def kernel(x, p, norm_mix, norm_ffn, norm_final, w_in, rw_w0, rw_w2, rw_a0, rw_a2, rw_g2, rw_kk, rw_ka, rw_rk, rw_ln_g, rw_ln_b, rw_mv_w1, rw_mv_w2, rw_mv_v0, gm_ln_g, gm_ln_b, gm_ws, gm_bs, w_proj_a, w_proj_b, w_out, peer_wq, peer_keys, peer_u, peer_v, ple_w, ple_gate):
    batch, seq, d = x.shape
    depth = w_in.shape[0]
    t = batch * seq
    w = RW_HEADS * RW_HEAD_DIM
    assert batch * RW_HEADS == LANES and batch == SUBLANES and d == w
    assert seq % GM_CHUNK == 0

    tm = _tile(t, 512)
    tt = _tile(t, 512)
    te = 2048
    steps = _tile(seq, 64)

    gw = gm_ln_g.shape[1]
    sizes = (w, w, w, rw_w2.shape[1], rw_a2.shape[1], rw_g2.shape[1], gw, gw, d, d)
    offs = [0]
    for sz in sizes:
        offs.append(offs[-1] + sz)

    def col(wi, idx):
        return wi[:, offs[idx]:offs[idx + 1]].astype(BF16)

    h = x.transpose(1, 0, 2).reshape(t, d)
    pt = p.transpose(0, 2, 1, 3).reshape(depth, t, p.shape[-1]).astype(BF16)
    row = lambda a: a.reshape(1, -1).astype(F32)
    expand = jnp.repeat(jnp.eye(GM_CHUNK, dtype=F32), batch, axis=0)
    same_batch = jnp.tile(jnp.eye(batch, dtype=F32), (GM_CHUNK, GM_CHUNK))
    tril = jnp.tril(jnp.ones((GM_CHUNK, GM_CHUNK), F32))

    v_first = None
    for i in range(depth):
        wi = w_in[i]
        nx = _norm_shift(h, norm_mix[i], batch, tm)

        r, k, kkraw, ar = _rk_call(nx, col(wi, 0), col(wi, 1), col(wi, 4), rw_a2[i].astype(BF16),
                                   row(rw_a0[i]), row(rw_kk[i]), row(rw_ka[i]), tm)
        mv = None
        if i > 0:
            mv = (rw_mv_w1[i - 1].astype(BF16), rw_mv_w2[i - 1].astype(BF16), row(rw_mv_v0[i - 1]))
        v, dec = _vw_call(nx, col(wi, 2), col(wi, 3), rw_w2[i].astype(BF16), row(rw_w0[i]), mv,
                          v_first, tm)
        if i == 0:
            v_first = v

        y = _wkv_call(r, dec, k, v, kkraw, ar,
                      _head_param_lanes(rw_rk[i], batch),
                      _head_param_lanes(rw_ln_g[i].reshape(RW_HEADS, RW_HEAD_DIM), batch),
                      _head_param_lanes(rw_ln_b[i].reshape(RW_HEADS, RW_HEAD_DIM), batch),
                      batch, steps)

        ws_k = (jnp.einsum("ri,gij,cj->grc", expand, gm_ws[i] * tril, expand)
                * same_batch).astype(BF16)
        bias = jnp.repeat(gm_bs[i].T, batch, axis=0).astype(F32)
        yb = _gmlp_call(nx, col(wi, 6), col(wi, 7), row(gm_ln_g[i]), row(gm_ln_b[i]), ws_k, bias)

        h, n2 = _merge_call(nx, y, yb, h, col(wi, 5), rw_g2[i].astype(BF16), col(wi, 8), col(wi, 9),
                            w_proj_a[i].astype(BF16), w_proj_b[i].astype(BF16),
                            w_out[i].astype(BF16), row(norm_ffn[i]), tm)

        take, rank, e1, e2, hp = _peer_select_call(
            n2, h, pt[i], peer_wq[i].astype(BF16), peer_keys[i].astype(BF16),
            ple_w[i].astype(BF16), ple_gate[i].astype(BF16), tt)
        h = _peer_dense_call(n2, peer_u[i].astype(BF16), peer_v[i].T.astype(BF16),
                             take, e1, rank, e2, hp, tt, te)

    return _final_norm_call(h.reshape(seq, batch * d), norm_final, batch, seq, _tile(seq, 512))
```

```python
import functools
import math

import jax
import jax.numpy as jnp
from jax import lax
from jax.experimental import pallas as pl
from jax.experimental.pallas import tpu as pltpu

F32 = jnp.float32
BF16 = jnp.bfloat16

RW_HEADS = 16
RW_HEAD_DIM = 64
RW_LN_EPS = 64e-5
GM_GROUPS = 8
GM_CHUNK = 128
GM_LN_EPS = 1e-5
PEER_HEADS = 8
PEER_KEYS = 128
PEER_HALF = 128
PEER_TOPK = 16
PEER_CHUNK = 256
RMS_EPS = 1e-6

LANES = 128
SUBLANES = 8
VMEM_LIMIT = 56 * 1024 * 1024


def _cparams(*sem):
    return pltpu.CompilerParams(dimension_semantics=tuple(sem), vmem_limit_bytes=VMEM_LIMIT)


def _const_spec(shape):
    nd = len(shape)
    return pl.BlockSpec(shape, lambda *_: (0,) * nd, pipeline_mode=pl.Buffered(1))


def _row_spec(rows, cols):
    return pl.BlockSpec((rows, cols), lambda i: (i, 0))


def _gelu(x):
    c = math.sqrt(2.0 / math.pi)
    return 0.5 * x * (1.0 + jnp.tanh(c * (x + 0.044715 * (x * x * x))))


def _gelu_x2(x):
    c = math.sqrt(2.0 / math.pi)
    return x * (1.0 + jnp.tanh(x * (c + (0.044715 * c) * (x * x))))


def _sigmoid(x):
    return 1.0 / (1.0 + jnp.exp(-x))


def _dot(a, b):
    return jnp.dot(a, b, preferred_element_type=F32)


def _rms(x, g):
    return x * lax.rsqrt(jnp.mean(x * x, axis=-1, keepdims=True) + RMS_EPS) * g


def _norm_shift_kernel(h_ref, hprev_ref, g_ref, nx_ref, *, batch):
    i = pl.program_id(0)
    g = g_ref[...]
    n = _rms(h_ref[...], g)
    prev = _rms(hprev_ref[...], g) * (i > 0).astype(F32)
    d = n.shape[1]
    shifted = jnp.concatenate([prev, n[:-batch]], axis=0)
    nx_ref[:, :d] = n.astype(BF16)
    nx_ref[:, d:] = shifted.astype(BF16)


def _norm_shift(h, g, batch, tm):
    t, d = h.shape
    per = tm // batch
    return pl.pallas_call(
        functools.partial(_norm_shift_kernel, batch=batch),
        grid=(t // tm,),
        in_specs=[_row_spec(tm, d),
                  pl.BlockSpec((batch, d), lambda i: (jnp.maximum(i * per - 1, 0), 0)),
                  _const_spec((1, d))],
        out_specs=_row_spec(tm, 2 * d),
        out_shape=jax.ShapeDtypeStruct((t, 2 * d), BF16),
        compiler_params=_cparams("parallel"),
        name="norm_shift",
    )(h, h, g.reshape(1, d))


def _rk_kernel(nx_ref, wr_ref, wk_ref, wa_ref, a2_ref, a0_ref, kk_ref, ka_ref,
               r_ref, k_ref, kkraw_ref, ar_ref):
    nx = nx_ref[...]
    r_ref[...] = _dot(nx, wr_ref[...])
    zk = _dot(nx, wk_ref[...])
    za = _dot(nx, wa_ref[...])
    a_rate = _sigmoid(a0_ref[...] + _dot(za.astype(BF16), a2_ref[...]))
    ar_ref[...] = a_rate
    kkraw_ref[...] = zk * kk_ref[...]
    k_ref[...] = zk * (1.0 + (a_rate - 1.0) * ka_ref[...])


def _rk_call(nx, wr, wk, wa, a2, a0, kk, ka, tm):
    t, d2 = nx.shape
    w = wr.shape[1]
    la = wa.shape[1]
    out = jax.ShapeDtypeStruct((t, w), F32)
    return pl.pallas_call(
        _rk_kernel,
        grid=(t // tm,),
        in_specs=[_row_spec(tm, d2), _const_spec((d2, w)), _const_spec((d2, w)),
                  _const_spec((d2, la)), _const_spec((la, w)),
                  _const_spec((1, w)), _const_spec((1, w)), _const_spec((1, w))],
        out_specs=[_row_spec(tm, w)] * 4,
        out_shape=[out] * 4,
        compiler_params=_cparams("parallel"),
        name="rwkv_rk",
    )(nx, wr, wk, wa, a2, a0, kk, ka)


def _decay_from(nx, ww_ref, w2_ref, w0_ref):
    zw = _dot(nx, ww_ref[...])
    x = -(w0_ref[...] + _dot(jnp.tanh(zw).astype(BF16), w2_ref[...]))
    softplus = jnp.maximum(x, 0.0) + jnp.log(1.0 + jnp.exp(-jnp.abs(x)))
    return jnp.exp(-jnp.exp(-softplus - 0.5))


def _vw0_kernel(nx_ref, wv_ref, ww_ref, w2_ref, w0_ref, v_ref, dec_ref):
    nx = nx_ref[...]
    v_ref[...] = _dot(nx, wv_ref[...])
    dec_ref[...] = _decay_from(nx, ww_ref, w2_ref, w0_ref)


def _vw_kernel(nx_ref, wv_ref, ww_ref, w2_ref, w0_ref, m1_ref, m2_ref, v0_ref, vfirst_ref,
               v_ref, dec_ref):
    nx = nx_ref[...]
    zv = _dot(nx, wv_ref[...])
    lo = _dot(nx, m1_ref[...])
    vmix = _sigmoid(v0_ref[...] + _dot(lo.astype(BF16), m2_ref[...]))
    v_ref[...] = zv + (vfirst_ref[...] - zv) * vmix
    dec_ref[...] = _decay_from(nx, ww_ref, w2_ref, w0_ref)


def _vw_call(nx, wv, ww, w2, w0, mv, vfirst, tm):
    t, d2 = nx.shape
    w = wv.shape[1]
    lw = ww.shape[1]
    out = jax.ShapeDtypeStruct((t, w), F32)
    in_specs = [_row_spec(tm, d2), _const_spec((d2, w)), _const_spec((d2, lw)),
                _const_spec((lw, w)), _const_spec((1, w))]
    args = [nx, wv, ww, w2, w0]
    if mv is None:
        body = _vw0_kernel
    else:
        m1, m2, v0 = mv
        lm = m1.shape[1]
        body = _vw_kernel
        in_specs += [_const_spec((d2, lm)), _const_spec((lm, w)), _const_spec((1, w)),
                     _row_spec(tm, w)]
        args += [m1, m2, v0, vfirst]
    return pl.pallas_call(
        body,
        grid=(t // tm,),
        in_specs=in_specs,
        out_specs=[_row_spec(tm, w)] * 2,
        out_shape=[out] * 2,
        compiler_params=_cparams("parallel"),
        name="rwkv_vw",
    )(*args)


def _to_lanes(blk):
    r = jnp.concatenate([blk[s * 8:(s + 1) * 8, j * LANES:(j + 1) * LANES]
                         for s in range(2) for j in range(8)], axis=0)
    tr = r.T
    lo, hi = tr[0:64], tr[64:128]
    lane = lax.broadcasted_iota(jnp.int32, (64, LANES), 1)
    first = lane < 64
    return (jnp.where(first, lo, pltpu.roll(hi, 64, 1)),
            jnp.where(first, pltpu.roll(lo, 64, 1), hi))


def _from_lanes(o0, o1):
    lane = lax.broadcasted_iota(jnp.int32, (64, LANES), 1)
    first = lane < 64
    lo = jnp.where(first, o0, pltpu.roll(o1, 64, 1))
    hi = jnp.where(first, pltpu.roll(o0, 64, 1), o1)
    r = jnp.concatenate([lo, hi], axis=0).T
    rows = [jnp.concatenate([r[s * 64 + j * 8:s * 64 + j * 8 + 8] for j in range(8)], axis=1)
            for s in range(2)]
    return jnp.concatenate(rows, axis=0)


_WR, _WW, _WK, _WV, _WA, _WB = range(6)


def _wkv_kernel(r_ref, w_ref, k_ref, v_ref, kk_ref, ar_ref, rk_ref, lng_ref, lnb_ref,
                y_ref, st_ref, buf_a, buf_b, y_a, y_b, *, steps):
    n = RW_HEAD_DIM
    pairs = steps // 2

    @pl.when(pl.program_id(0) == 0)
    def _():
        st_ref[...] = jnp.zeros_like(st_ref)

    def load_pair(sp, buf):
        rows = pl.ds(pl.multiple_of(sp * 16, 16), 16)
        for idx, src in ((_WR, r_ref), (_WW, w_ref), (_WK, k_ref), (_WV, v_ref)):
            o0, o1 = _to_lanes(src[rows, :])
            buf[idx, 0] = o0
            buf[idx, 1] = o1
        kk0, kk1 = _to_lanes(kk_ref[rows, :])
        ar0, ar1 = _to_lanes(ar_ref[rows, :])
        for off, kk, ar in ((0, kk0, ar0), (1, kk1, ar1)):
            ss = jnp.sum(kk * kk, axis=0, keepdims=True)
            kkn = kk * lax.rsqrt(jnp.maximum(ss, 1e-24))
            buf[_WA, off] = -kkn
            buf[_WB, off] = kkn * ar

    def step(buf, off, nbuf, noff, ybuf, sa):
        nxt = []
        hv = n // 4
        for half in range(n // hv):
            vs = slice(half * hv, (half + 1) * hv)
            sa_h = sa[vs]
            vv = buf[_WV, off, vs, :]
            y = jnp.zeros((hv, LANES), F32)
            san = jnp.zeros((hv, LANES), F32)
            for k in range(n):
                row = slice(k, k + 1)
                sk = (st_ref[k, vs, :] * buf[_WW, off, row, :] + sa_h * buf[_WB, off, row, :]
                      + vv * buf[_WK, off, row, :])
                st_ref[k, vs, :] = sk
                y = y + sk * buf[_WR, off, row, :]
                san = san + sk * nbuf[_WA, noff, row, :]
            ybuf[off, vs, :] = y
            nxt.append(san)
        return jnp.concatenate(nxt, axis=0)

    rk = rk_ref[...]
    lng = lng_ref[...]
    lnb = lnb_ref[...]

    def finish(buf, off, ybuf):
        y = ybuf[off]
        mu = jnp.mean(y, axis=0, keepdims=True)
        yc = y - mu
        var = jnp.mean(yc * yc, axis=0, keepdims=True)
        yn = yc * lax.rsqrt(var + RW_LN_EPS) * lng + lnb
        bonus = jnp.sum(buf[_WR, off] * buf[_WK, off] * rk, axis=0, keepdims=True)
        return yn + bonus * buf[_WV, off]

    def store_pair(sp, buf, ybuf):
        rows = pl.ds(pl.multiple_of(sp * 16, 16), 16)
        y_ref[rows, :] = _from_lanes(finish(buf, 0, ybuf), finish(buf, 1, ybuf))

    load_pair(0, buf_a)
    sa0 = jnp.zeros((n, LANES), F32)
    for k in range(n):
        sa0 = sa0 + st_ref[k] * buf_a[_WA, 0, k:k + 1, :]

    def two_pairs(q, sa):
        sa = step(buf_a, 0, buf_a, 1, y_a, sa)
        load_pair(2 * q + 1, buf_b)
        sa = step(buf_a, 1, buf_b, 0, y_a, sa)
        sa = step(buf_b, 0, buf_b, 1, y_b, sa)
        store_pair(2 * q, buf_a, y_a)
        load_pair(jnp.minimum(2 * q + 2, pairs - 1), buf_a)
        sa = step(buf_b, 1, buf_a, 0, y_b, sa)
        store_pair(2 * q + 1, buf_b, y_b)
        return sa

    lax.fori_loop(0, pairs // 2, two_pairs, sa0)


def _wkv_call(r, w, k, v, kkraw, ar, rk_l, lng_l, lnb_l, batch, steps):
    t, width = r.shape
    rows = steps * batch
    n = RW_HEAD_DIM
    assert steps % 4 == 0
    pair_buf = pltpu.VMEM((6, 2, n, LANES), F32)
    pair_out = pltpu.VMEM((2, n, LANES), F32)
    return pl.pallas_call(
        functools.partial(_wkv_kernel, steps=steps),
        grid=(t // rows,),
        in_specs=[_row_spec(rows, width)] * 6 + [_const_spec((n, LANES))] * 3,
        out_specs=_row_spec(rows, width),
        out_shape=jax.ShapeDtypeStruct((t, width), F32),
        scratch_shapes=[pltpu.VMEM((n, n, LANES), F32), pair_buf, pair_buf, pair_out, pair_out],
        compiler_params=_cparams("arbitrary"),
        name="wkv7_scan",
    )(r, w, k, v, kkraw, ar, rk_l, lng_l, lnb_l)


def _head_param_lanes(pv, batch):
    a = pv.reshape(RW_HEADS // 2, 2, RW_HEAD_DIM).transpose(2, 1, 0)
    a = jnp.broadcast_to(a[..., None], a.shape + (batch,))
    return a.reshape(RW_HEAD_DIM, LANES).astype(F32)


def _gmlp_kernel(nx_ref, wu_ref, wg_ref, lng_ref, lnb_ref, ws_ref, bias_ref, yb_ref):
    nx = nx_ref[...]
    u = _gelu(_dot(nx, wu_ref[...]))
    gv = _gelu(_dot(nx, wg_ref[...]))
    mu = jnp.mean(gv, axis=-1, keepdims=True)
    gc = gv - mu
    var = jnp.mean(gc * gc, axis=-1, keepdims=True)
    v = (gc * lax.rsqrt(var + GM_LN_EPS) * lng_ref[...] + lnb_ref[...]).astype(BF16)
    gd = v.shape[1] // GM_GROUPS
    for g in range(GM_GROUPS):
        cols = slice(g * gd, (g + 1) * gd)
        s = _dot(ws_ref[g], v[:, cols]) + bias_ref[:, g:g + 1]
        yb_ref[:, cols] = (u[:, cols] * s).astype(BF16)


def _gmlp_call(nx, wu, wg, lng, lnb, ws_k, bias):
    t, d2 = nx.shape
    w = wu.shape[1]
    rows = ws_k.shape[1]
    return pl.pallas_call(
        _gmlp_kernel,
        grid=(t // rows,),
        in_specs=[_row_spec(rows, d2), _const_spec((d2, w)), _const_spec((d2, w)),
                  _const_spec((1, w)), _const_spec((1, w)),
                  _const_spec(ws_k.shape), _const_spec((rows, GM_GROUPS))],
        out_specs=_row_spec(rows, w),
        out_shape=jax.ShapeDtypeStruct((t, w), BF16),
        compiler_params=_cparams("parallel"),
        name="gmlp",
    )(nx, wu, wg, lng, lnb, ws_k, bias)


def _merge_kernel(nx_ref, y_ref, yb_ref, h_ref, wg_ref, g2_ref, wga_ref, wgb_ref,
                  wa_ref, wb_ref, wo_ref, nf_ref, hn_ref, n2_ref):
    nx = nx_ref[...]
    gate = _dot(_sigmoid(_dot(nx, wg_ref[...])).astype(BF16), g2_ref[...])
    ya = (y_ref[...] * gate).astype(BF16)
    pa = _dot(ya, wa_ref[...])
    pb = _dot(yb_ref[...], wb_ref[...])
    merged = _sigmoid(_dot(nx, wga_ref[...])) * pa + _sigmoid(_dot(nx, wgb_ref[...])) * pb
    hn = h_ref[...] + _dot(merged.astype(BF16), wo_ref[...])
    hn_ref[...] = hn
    n2_ref[...] = _rms(hn, nf_ref[...]).astype(BF16)


def _merge_call(nx, y, yb, h, wg, g2, wga, wgb, wa, wb, wo, nf, tm):
    t, d2 = nx.shape
    d = h.shape[1]
    w = y.shape[1]
    lg = wg.shape[1]
    return pl.pallas_call(
        _merge_kernel,
        grid=(t // tm,),
        in_specs=[_row_spec(tm, d2), _row_spec(tm, w), _row_spec(tm, w), _row_spec(tm, d),
                  _const_spec((d2, lg)), _const_spec((lg, w)),
                  _const_spec((d2, d)), _const_spec((d2, d)),
                  _const_spec((w, d)), _const_spec((w, d)), _const_spec((d, d)),
                  _const_spec((1, d))],
        out_specs=[_row_spec(tm, d), _row_spec(tm, d)],
        out_shape=[jax.ShapeDtypeStruct((t, d), F32), jax.ShapeDtypeStruct((t, d), BF16)],
        compiler_params=_cparams("parallel"),
        name="merge",
    )(nx, y, yb, h, wg, g2, wga, wgb, wa, wb, wo, nf)


def _bitonic_exchanges(n):
    out = []
    k = 2
    while k <= n:
        j = k // 2
        while j >= 1:
            for i in range(n):
                l = i ^ j
                if l > i:
                    out.append((i, l, (i & k) == 0))
            j //= 2
        k *= 2
    return out


def _top_values(s, count, rows):
    slabs = [s[i:i + SUBLANES] for i in range(0, s.shape[0], SUBLANES)]
    wires = 1
    while wires < len(slabs):
        wires *= 2
    slabs += [None] * (wires - len(slabs))
    for i, j, ascending in _bitonic_exchanges(wires):
        a, b = slabs[i], slabs[j]
        if a is None and b is None:
            continue
        if a is None or b is None:
            hi, lo = (a if b is None else b), None
        else:
            hi, lo = jnp.maximum(a, b), jnp.minimum(a, b)
        slabs[i], slabs[j] = (lo, hi) if ascending else (hi, lo)
    lists = slabs[::-1]
    neg = jnp.full((SUBLANES, s.shape[1]), -jnp.inf, s.dtype)
    lists = [neg if x is None else x for x in lists]
    tops = []
    for it in range(count):
        m = jnp.max(lists[0], axis=0, keepdims=True)
        tops.append(m)
        hit = lists[0] == m
        depth = min(count - it - 1, len(lists))
        for j in range(depth):
            below = lists[j + 1] if j + 1 < len(lists) else neg
            lists[j] = jnp.where(hit, below, lists[j])
    tops += [jnp.full_like(tops[0], -jnp.inf)] * (rows - count)
    return jnp.concatenate(tops, axis=0)


def _peer_select_kernel(n2_ref, h_ref, p_ref, wq_ref, keys_ref, wple_ref, wpg_ref,
                        take_ref, rank_ref, e1_ref, e2_ref, hp_ref, q_scr):
    n2 = n2_ref[...]
    ple = _dot(p_ref[...], wple_ref[...]) * _sigmoid(_dot(n2, wpg_ref[...]))
    hp_ref[...] = h_ref[...] + ple
    q = _dot(n2, wq_ref[...]).astype(BF16)
    for hp in range(2 * PEER_HEADS):
        q_scr[hp] = q[:, hp * PEER_HALF:(hp + 1) * PEER_HALF]
    tt = n2.shape[0]
    kk = PEER_TOPK
    nt = (((1,), (1,)), ((), ()))

    def head(h, _):
        s1 = lax.dot_general(keys_ref[h, 0], q_scr[2 * h], nt, preferred_element_type=F32)
        s2 = lax.dot_general(keys_ref[h, 1], q_scr[2 * h + 1], nt, preferred_element_type=F32)
        per = PEER_CHUNK // LANES
        pad = 3 * SUBLANES
        for c in range(tt // LANES):
            cols = slice(c * LANES, (c + 1) * LANES)
            dst = (h, c // per, slice(None), slice((c % per) * LANES, (c % per + 1) * LANES))
            s1c = s1[:, cols]
            s2c = s2[:, cols]
            t1 = _top_values(s1c, kk + 1, pad)
            t2 = _top_values(s2c, kk + 1, pad)
            cand = [t1[0:1] + t2]
            cand += [t1[i:i + 1] + t2[0:SUBLANES] for i in range(1, SUBLANES)]
            cand += [t1[SUBLANES:pad] + t2[0:1]]
            best = _top_values(jnp.concatenate(cand, axis=0), kk + 1, kk + 1)
            top = best[0:kk]
            inv_z = 1.0 / jnp.sum(jnp.exp(top - top[0:1]), axis=0, keepdims=True)
            tau = 0.5 * (best[kk - 1:kk] + best[kk:kk + 1])
            thr = tau - s1c
            take = jnp.zeros_like(s1c)
            rank = jnp.zeros_like(s2c)
            for j in range(kk):
                tj = t2[j:j + 1]
                take = jnp.where(tj >= thr, j + 1.0, take)
                rank = jnp.where(tj > s2c, j + 1.0, rank)
            take_ref[dst] = take
            rank_ref[dst] = rank.astype(BF16)
            e1_ref[dst] = jnp.exp(s1c - t1[0:1])
            e2_ref[dst] = (jnp.exp(s2c - t2[0:1]) * (0.5 * inv_z)).astype(BF16)
        return 0

    lax.fori_loop(0, PEER_HEADS, head, 0)


def _peer_select_call(n2, h, p, wq, keys, wple, wpg, tt):
    t, d = n2.shape
    pd = p.shape[1]
    dq = wq.shape[1]
    nh, nk = PEER_HEADS, PEER_KEYS
    tspec = pl.BlockSpec((nh, tt // PEER_CHUNK, nk, PEER_CHUNK), lambda i: (0, i, 0, 0))
    oshape = (nh, t // PEER_CHUNK, nk, PEER_CHUNK)
    tshapes = [jax.ShapeDtypeStruct(oshape, dt) for dt in (F32, BF16, F32, BF16)]
    return pl.pallas_call(
        _peer_select_kernel,
        grid=(t // tt,),
        in_specs=[_row_spec(tt, d), _row_spec(tt, d), _row_spec(tt, pd),
                  _const_spec((d, dq)), _const_spec(keys.shape),
                  _const_spec((pd, d)), _const_spec((d, d))],
        out_specs=[tspec] * 4 + [_row_spec(tt, d)],
        out_shape=tshapes + [jax.ShapeDtypeStruct((t, d), F32)],
        scratch_shapes=[pltpu.VMEM((2 * nh, tt, PEER_HALF), BF16)],
        compiler_params=_cparams("parallel"),
        name="peer_select",
    )(n2, h, p, wq, keys, wple, wpg)


def _peer_dense_kernel(n2_ref, u_ref, vt_ref, take_ref, e1_ref, rank_ref, e2_ref, hp_ref,
                       out_ref, acc_ref, act0_ref, act1_ref, ga0_ref, ga1_ref, *, nj):
    f = pl.program_id(0)
    nt = (((1,), (1,)), ((), ()))
    nc, te, cw = act0_ref.shape
    nk = PEER_KEYS
    rb = 2 * SUBLANES
    j3 = jnp.maximum(f - 2, 0) % nj

    @pl.when(f == 0)
    def _():
        act1_ref[...] = jnp.zeros_like(act1_ref)
        ga0_ref[...] = jnp.zeros_like(ga0_ref)

    @pl.when(j3 == 0)
    def _():
        acc_ref[...] = jnp.zeros_like(acc_ref)

    def stages(act_new, act_cur, ga_cur, ga_old):
        for c in range(nc):
            tok = slice(c * cw, (c + 1) * cw)
            act_new[c] = _gelu_x2(lax.dot_general(u_ref[...], n2_ref[tok, :], nt,
                                                  preferred_element_type=F32).astype(BF16))
            for i1 in range(te // nk):
                def row16(ref, h):
                    return jnp.broadcast_to(ref[h, c, i1:i1 + 1, :], (rb, cw)).astype(BF16)

                take = [row16(take_ref, h) for h in range(PEER_HEADS)]
                e1 = [row16(e1_ref, h) for h in range(PEER_HEADS)]
                zero = jnp.zeros((rb, cw), BF16)
                for r in range(nk // rb):
                    rs = slice(r * rb, (r + 1) * rb)
                    g = None
                    for h in range(PEER_HEADS):
                        keep = rank_ref[h, c, rs, :] < take[h]
                        term = jnp.where(keep, e2_ref[h, c, rs, :], zero) * e1[h]
                        g = term if g is None else g + term
                    rows = slice(i1 * nk + r * rb, i1 * nk + (r + 1) * rb)
                    ga_cur[c, rows, :] = g * act_cur[c, rows, :]
            acc_ref[c] += _dot(vt_ref[...], ga_old[c])

    @pl.when(f % 2 == 0)
    def _():
        stages(act0_ref, act1_ref, ga1_ref, ga0_ref)

    @pl.when(f % 2 == 1)
    def _():
        stages(act1_ref, act0_ref, ga0_ref, ga1_ref)

    @pl.when(jnp.logical_and(f >= 2, j3 == nj - 1))
    def _():
        for c in range(nc):
            out_ref[c * cw:(c + 1) * cw, :] = hp_ref[c * cw:(c + 1) * cw, :] + acc_ref[c].T


def _peer_dense_call(n2, u, vt, take, e1, rank, e2, hp, tt, te):
    t, d = n2.shape
    ne = u.shape[0]
    nh, nk = PEER_HEADS, PEER_KEYS
    r1 = te // nk
    cw = PEER_CHUNK
    nc = tt // cw
    ni, nj = t // tt, ne // te
    last = ni * nj - 1

    def pair(f, lag):
        a = jnp.clip(f - lag, 0, last)
        return a // nj, a % nj

    score_blk = (nh, nc, nk, cw)
    row_blk = (nh, nc, r1, cw)
    return pl.pallas_call(
        functools.partial(_peer_dense_kernel, nj=nj),
        grid=(ni * nj + 2,),
        in_specs=[pl.BlockSpec((tt, d), lambda f: (pair(f, 0)[0], 0)),
                  pl.BlockSpec((te, d), lambda f: (pair(f, 0)[1], 0)),
                  pl.BlockSpec((d, te), lambda f: (0, pair(f, 2)[1])),
                  pl.BlockSpec(row_blk, lambda f: (0, pair(f, 1)[0], pair(f, 1)[1], 0)),
                  pl.BlockSpec(row_blk, lambda f: (0, pair(f, 1)[0], pair(f, 1)[1], 0)),
                  pl.BlockSpec(score_blk, lambda f: (0, pair(f, 1)[0], 0, 0)),
                  pl.BlockSpec(score_blk, lambda f: (0, pair(f, 1)[0], 0, 0)),
                  pl.BlockSpec((tt, d), lambda f: (pair(f, 2)[0], 0))],
        out_specs=pl.BlockSpec((tt, d), lambda f: (pair(f, 2)[0], 0)),
        out_shape=jax.ShapeDtypeStruct((t, d), F32),
        scratch_shapes=[pltpu.VMEM((nc, d, cw), F32)]
        + [pltpu.VMEM((nc, te, cw), BF16)] * 4,
        compiler_params=_cparams("arbitrary"),
        name="peer_dense",
    )(n2, u, vt, take, e1, rank, e2, hp)


def _final_norm_kernel(h_ref, g_ref, o_ref):
    o_ref[0] = _rms(h_ref[...], g_ref[...])


def _final_norm_call(h2d, g, batch, seq, ts):
    d = g.shape[-1]
    return pl.pallas_call(
        _final_norm_kernel,
        grid=(seq // ts, batch),
        in_specs=[pl.BlockSpec((ts, d), lambda s, b: (s, b)), _const_spec((1, d))],
        out_specs=pl.BlockSpec((1, ts, d), lambda s, b: (b, s, 0)),
        out_shape=jax.ShapeDtypeStruct((batch, seq, d), F32),
        compiler_params=_cparams("parallel", "parallel"),
        name="final_norm",
    )(h2d, g.reshape(1, d))


def _tile(total, want):
    want = min(want, total)
    assert total % want == 0, (total, want)
    return want


def kernel(x, p, norm_mix, norm_ffn, norm_final, w_in, rw_w0, rw_w2, rw_a0, rw_a2, rw_g2, rw_kk, rw_ka, rw_rk, rw_ln_g, rw_ln_b, rw_mv_w1, rw_mv_w2, rw_mv_v0, gm_ln_g, gm_ln_b, gm_ws, gm_bs, w_proj_a, w_proj_b, w_out, peer_wq, peer_keys, peer_u, peer_v, ple_w, ple_gate):
    batch, seq, d = x.shape
    depth = w_in.shape[0]
    t = batch * seq
    w = RW_HEADS * RW_HEAD_DIM
    assert batch * RW_HEADS == LANES and batch == SUBLANES and d == w
    assert seq % GM_CHUNK == 0

    tm = _tile(t, 512)
    tt = _tile(t, 512)
    te = 2048
    steps = _tile(seq, 64)

    gw = gm_ln_g.shape[1]
    sizes = (w, w, w, rw_w2.shape[1], rw_a2.shape[1], rw_g2.shape[1], gw, gw, d, d)
    offs = [0]
    for sz in sizes:
        offs.append(offs[-1] + sz)

    def col(wi, idx):
        return wi[:, offs[idx]:offs[idx + 1]].astype(BF16)

    h = x.transpose(1, 0, 2).reshape(t, d)
    pt = p.transpose(0, 2, 1, 3).reshape(depth, t, p.shape[-1]).astype(BF16)
    row = lambda a: a.reshape(1, -1).astype(F32)
    expand = jnp.repeat(jnp.eye(GM_CHUNK, dtype=F32), batch, axis=0)
    same_batch = jnp.tile(jnp.eye(batch, dtype=F32), (GM_CHUNK, GM_CHUNK))
    tril = jnp.tril(jnp.ones((GM_CHUNK, GM_CHUNK), F32))

    v_first = None
    for i in range(depth):
        wi = w_in[i]
        nx = _norm_shift(h, norm_mix[i], batch, tm)

        r, k, kkraw, ar = _rk_call(nx, col(wi, 0), col(wi, 1), col(wi, 4), rw_a2[i].astype(BF16),
                                   row(rw_a0[i]), row(rw_kk[i]), row(rw_ka[i]), tm)
        mv = None
        if i > 0:
            mv = (rw_mv_w1[i - 1].astype(BF16), rw_mv_w2[i - 1].astype(BF16), row(rw_mv_v0[i - 1]))
        v, dec = _vw_call(nx, col(wi, 2), col(wi, 3), rw_w2[i].astype(BF16), row(rw_w0[i]), mv,
                          v_first, tm)
        if i == 0:
            v_first = v

        y = _wkv_call(r, dec, k, v, kkraw, ar,
                      _head_param_lanes(rw_rk[i], batch),
                      _head_param_lanes(rw_ln_g[i].reshape(RW_HEADS, RW_HEAD_DIM), batch),
                      _head_param_lanes(rw_ln_b[i].reshape(RW_HEADS, RW_HEAD_DIM), batch),
                      batch, steps)

        ws_k = (jnp.einsum("ri,gij,cj->grc", expand, gm_ws[i] * tril, expand)
                * same_batch).astype(BF16)
        bias = jnp.repeat(gm_bs[i].T, batch, axis=0).astype(F32)
        yb = _gmlp_call(nx, col(wi, 6), col(wi, 7), row(gm_ln_g[i]), row(gm_ln_b[i]), ws_k, bias)

        h, n2 = _merge_call(nx, y, yb, h, col(wi, 5), rw_g2[i].astype(BF16), col(wi, 8), col(wi, 9),
                            w_proj_a[i].astype(BF16), w_proj_b[i].astype(BF16),
                            w_out[i].astype(BF16), row(norm_ffn[i]), tm)

        take, rank, e1, e2, hp = _peer_select_call(
            n2, h, pt[i], peer_wq[i].astype(BF16), peer_keys[i].astype(BF16),
            ple_w[i].astype(BF16), ple_gate[i].astype(BF16), tt)
        h = _peer_dense_call(n2, peer_u[i].astype(BF16), peer_v[i].T.astype(BF16),
                             take, e1, rank, e2, hp, tt, te)

    return _final_norm_call(h.reshape(seq, batch * d), norm_final, batch, seq, _tile(seq, 512))
```
